```python
import jax
import jax.numpy as jnp
from jax import lax
import numpy as np

D_MODEL = 1024
BATCH = 2
SEQ = 8192
DEPTH = 4
DEC_BATCH = 32
DEC_SEQ = 8
PAST_LEN = 8192
PAGE_SIZE = 128

N_MIXERS = 3
N_CONV_LAYERS = (DEPTH + 2) // 3
N_GLA_LAYERS = (DEPTH + 1) // 3
N_FOX_LAYERS = DEPTH // 3

CONV_WIDTH = 31
CONV_DIM = D_MODEL
GLA_HEADS = 4
GLA_DK = D_MODEL // 2
GLA_DV = D_MODEL
GLA_HEAD_K = GLA_DK // GLA_HEADS
GLA_HEAD_V = GLA_DV // GLA_HEADS
GLA_GATE_RANK = 16
GLA_GATE_NORM = 16.0
GLA_CHUNK = 64
FOX_HEADS = 16
FOX_HEAD_DIM = D_MODEL // FOX_HEADS
FOX_BLOCK = 128
FFN_DIM = 2816
FFN_CONV_WIDTH = 3
EPS = 1e-6

kernel_name = 'hybrid_conv_gla_fox_convffn_step'


def rmsnorm(x, g):
    xf = x.astype(jnp.float32)
    y = xf * lax.rsqrt(jnp.mean(xf * xf, axis=-1, keepdims=True) + EPS)
    return (y * g.astype(jnp.float32)).astype(x.dtype)


def layernorm(x, g, b):
    xf = x.astype(jnp.float32)
    mu = jnp.mean(xf, axis=-1, keepdims=True)
    var = jnp.mean(jnp.square(xf - mu), axis=-1, keepdims=True)
    y = (xf - mu) * lax.rsqrt(var + EPS)
    return (y * g.astype(jnp.float32) + b.astype(jnp.float32)).astype(x.dtype)


def causal_dwconv(x, buf, w, b):
    W, C = w.shape
    xp = jnp.concatenate([buf.astype(x.dtype), x], axis=1)
    y = lax.conv_general_dilated(xp, w[:, None, :].astype(x.dtype), window_strides=(1,), padding='VALID',
                                 dimension_numbers=('NWC', 'WIO', 'NWC'), feature_group_count=C)
    return y + b.astype(x.dtype), xp[:, -(W - 1):]


def paged_gather(pool, page_table):
    rows = pool[page_table]
    return rows.reshape((page_table.shape[0], page_table.shape[1] * pool.shape[1]) + pool.shape[2:])


def conv_mixer(x, buf, w_in, w_dw, b_dw, ln_g, ln_b, w_out):
    a, gt = jnp.split(x @ w_in, 2, axis=-1)
    u = a * jax.nn.sigmoid(gt)
    u, new_buf = causal_dwconv(u, buf, w_dw, b_dw)
    u = jax.nn.silu(layernorm(u, ln_g, ln_b))
    return u @ w_out, new_buf


def gla_recurrence(q, k, v, g, S0, chunk):
    B, L, H, K = q.shape
    V = v.shape[-1]
    n = L // chunk

    def to_chunks(t):
        return t.astype(jnp.float32).reshape(B, n, chunk, H, t.shape[-1]).swapaxes(0, 1)

    causal = jnp.tril(jnp.ones((chunk, chunk), bool))

    def step(S, inp):
        qc, kc, vc, gc = inp
        G = jnp.cumsum(gc, axis=1)
        G_last = G[:, -1]
        qe = qc * jnp.exp(G)
        ke = kc * jnp.exp(-G)
        scores = jnp.where(causal, jnp.einsum('bthk,bshk->bhts', qe, ke), 0.0)
        o = jnp.einsum('bthk,bhkv->bthv', qe, S) + jnp.einsum('bhts,bshv->bthv', scores, vc)
        kd = kc * jnp.exp(G_last[:, None] - G)
        S = S * jnp.exp(G_last)[..., None] + jnp.einsum('bshk,bshv->bhkv', kd, vc)
        return S, o

    S, o = lax.scan(step, S0.astype(jnp.float32), (to_chunks(q), to_chunks(k), to_chunks(v), to_chunks(g)))
    return o.swapaxes(0, 1).reshape(B, L, H, V), S


def gla_mixer(x, S0, w_in, w_gk2, b_gk, norm_g, w_out):
    B, L, _ = x.shape
    h = x @ w_in
    q, k, v, r, gk1 = jnp.split(h, [GLA_DK, 2 * GLA_DK, 2 * GLA_DK + GLA_DV, 2 * GLA_DK + 2 * GLA_DV], axis=-1)
    gk = jax.nn.log_sigmoid((gk1 @ w_gk2 + b_gk).astype(jnp.float32)) / GLA_GATE_NORM
    q = q.reshape(B, L, GLA_HEADS, GLA_HEAD_K) * (GLA_HEAD_K ** -0.5)
    k = k.reshape(B, L, GLA_HEADS, GLA_HEAD_K)
    v = v.reshape(B, L, GLA_HEADS, GLA_HEAD_V)
    gk = gk.reshape(B, L, GLA_HEADS, GLA_HEAD_K)
    chunk = GLA_CHUNK if L % GLA_CHUNK == 0 else L
    o, S = gla_recurrence(q, k, v, gk, S0, chunk)
    o = rmsnorm(o.astype(x.dtype), norm_g).reshape(B, L, GLA_DV) * jax.nn.silu(r)
    return o @ w_out, S.astype(S0.dtype)


def fox_attention(q, k, v, cq, ck, q_offset):
    B, Lq, H, HD = q.shape
    Lk = k.shape[1]
    blk = FOX_BLOCK if Lq % FOX_BLOCK == 0 else Lq
    nb = Lq // blk
    qb = q.reshape(B, nb, blk, H, HD).swapaxes(0, 1)
    cqb = cq.reshape(B, nb, blk, H).swapaxes(0, 1)
    qpos = (q_offset + jnp.arange(Lq)).reshape(nb, blk)
    kpos = jnp.arange(Lk)
    ckT = ck.transpose(0, 2, 1)[:, :, None, :]
    scale = HD ** -0.5

    def one_block(args):
        qi, cqi, pi = args
        s = jnp.einsum('bqhd,bkhd->bhqk', qi, k, preferred_element_type=jnp.float32) * scale
        s = s + (cqi.transpose(0, 2, 1)[..., None] - ckT)
        s = jnp.where(kpos[None, :] <= pi[:, None], s, -jnp.inf)
        p = jax.nn.softmax(s, axis=-1)
        return jnp.einsum('bhqk,bkhd->bqhd', p.astype(v.dtype), v)

    o = lax.map(one_block, (qb, cqb, qpos))
    return o.swapaxes(0, 1).reshape(B, Lq, H, HD)


def fox_mixer(x, k_past, v_past, lf_past, w_in, b_f, qn_g, kn_g, w_out):
    B, L, _ = x.shape
    D = D_MODEL
    h = x @ w_in
    q, k, v, og, fl = jnp.split(h, [D, 2 * D, 3 * D, 4 * D], axis=-1)
    q = rmsnorm(q.reshape(B, L, FOX_HEADS, FOX_HEAD_DIM), qn_g)
    k = rmsnorm(k.reshape(B, L, FOX_HEADS, FOX_HEAD_DIM), kn_g)
    v = v.reshape(B, L, FOX_HEADS, FOX_HEAD_DIM)
    logf = jax.nn.log_sigmoid((fl + b_f).astype(jnp.float32))
    k_all = jnp.concatenate([k_past.astype(k.dtype), k], axis=1)
    v_all = jnp.concatenate([v_past.astype(v.dtype), v], axis=1)
    c = jnp.cumsum(jnp.concatenate([lf_past.astype(jnp.float32), logf], axis=1), axis=1)
    o = fox_attention(q, k_all, v_all, c[:, -L:], c, k_past.shape[1])
    o = o.reshape(B, L, D) * jax.nn.sigmoid(og)
    return o @ w_out, k, v, logf.astype(x.dtype)


def conv_ffn(x, buf, w_up, w_dw, b_dw, w_down):
    u, new_buf = causal_dwconv(x @ w_up, buf, w_dw, b_dw)
    a, b = jnp.split(u, 2, axis=-1)
    return (jax.nn.silu(a) * b) @ w_down, new_buf


def setup_inputs(seed: int = 0) -> dict:
    key = jax.random.key(seed)
    ks = iter(jax.random.split(key, 40))

    def nrm(shape, scale=1.0):
        return jax.random.normal(next(ks), shape, jnp.float32) * scale

    D = D_MODEL
    n_pages = PAST_LEN // PAGE_SIZE
    n_pool = (DEC_BATCH * n_pages * 5) // 4
    page_table = jax.random.permutation(next(ks), n_pool)[:DEC_BATCH * n_pages].reshape(DEC_BATCH, n_pages).astype(jnp.int32)
    gla_in_w = 2 * GLA_DK + 2 * GLA_DV + GLA_GATE_RANK
    fox_in_w = 4 * D + FOX_HEADS
    return {
        'x_prompt': nrm((BATCH, SEQ, D)),
        'x_sample': nrm((DEC_BATCH, DEC_SEQ, D)),
        'state_conv': nrm((N_CONV_LAYERS, DEC_BATCH, CONV_WIDTH - 1, CONV_DIM), 0.5),
        'state_gla': nrm((N_GLA_LAYERS, DEC_BATCH, GLA_HEADS, GLA_HEAD_K, GLA_HEAD_V), 0.5),
        'cache_fox_k': nrm((N_FOX_LAYERS, n_pool, PAGE_SIZE, FOX_HEADS, FOX_HEAD_DIM)),
        'cache_fox_v': nrm((N_FOX_LAYERS, n_pool, PAGE_SIZE, FOX_HEADS, FOX_HEAD_DIM)),
        'cache_fox_logf': jax.nn.log_sigmoid(3.0 + nrm((N_FOX_LAYERS, n_pool, PAGE_SIZE, FOX_HEADS))),
        'state_ffn_conv': nrm((DEPTH, DEC_BATCH, FFN_CONV_WIDTH - 1, 2 * FFN_DIM), 0.5),
        'page_table': page_table,
        'norm_mix_g': 1.0 + nrm((DEPTH, D), 0.1),
        'norm_ffn_g': 1.0 + nrm((DEPTH, D), 0.1),
        'w_conv_in': nrm((N_CONV_LAYERS, D, 2 * CONV_DIM), D ** -0.5),
        'w_conv_dw': nrm((N_CONV_LAYERS, CONV_WIDTH, CONV_DIM), CONV_WIDTH ** -0.5),
        'b_conv_dw': nrm((N_CONV_LAYERS, CONV_DIM), 0.02),
        'conv_ln_g': 1.0 + nrm((N_CONV_LAYERS, CONV_DIM), 0.1),
        'conv_ln_b': nrm((N_CONV_LAYERS, CONV_DIM), 0.02),
        'w_conv_out': nrm((N_CONV_LAYERS, CONV_DIM, D), CONV_DIM ** -0.5),
        'w_gla_in': nrm((N_GLA_LAYERS, D, gla_in_w), D ** -0.5),
        'w_gla_gk2': nrm((N_GLA_LAYERS, GLA_GATE_RANK, GLA_DK), GLA_GATE_RANK ** -0.5),
        'b_gla_gk': nrm((N_GLA_LAYERS, GLA_DK), 0.1),
        'gla_norm_g': 1.0 + nrm((N_GLA_LAYERS, GLA_HEAD_V), 0.1),
        'w_gla_out': nrm((N_GLA_LAYERS, GLA_DV, D), GLA_DV ** -0.5),
        'w_fox_in': nrm((N_FOX_LAYERS, D, fox_in_w), D ** -0.5),
        'b_fox_f': 3.0 + nrm((N_FOX_LAYERS, FOX_HEADS), 0.5),
        'fox_qn_g': 1.0 + nrm((N_FOX_LAYERS, FOX_HEAD_DIM), 0.1),
        'fox_kn_g': 1.0 + nrm((N_FOX_LAYERS, FOX_HEAD_DIM), 0.1),
        'w_fox_out': nrm((N_FOX_LAYERS, D, D), D ** -0.5),
        'w_ffn_up': nrm((DEPTH, D, 2 * FFN_DIM), D ** -0.5),
        'w_ffn_dw': nrm((DEPTH, FFN_CONV_WIDTH, 2 * FFN_DIM), FFN_CONV_WIDTH ** -0.5),
        'b_ffn_dw': nrm((DEPTH, 2 * FFN_DIM), 0.02),
        'w_ffn_down': nrm((DEPTH, FFN_DIM, D), FFN_DIM ** -0.5),
    }


def reference(x_prompt, x_sample, state_conv, state_gla, cache_fox_k, cache_fox_v, cache_fox_logf, state_ffn_conv, page_table,
              norm_mix_g, norm_ffn_g, w_conv_in, w_conv_dw, b_conv_dw, conv_ln_g, conv_ln_b, w_conv_out,
              w_gla_in, w_gla_gk2, b_gla_gk, gla_norm_g, w_gla_out,
              w_fox_in, b_fox_f, fox_qn_g, fox_kn_g, w_fox_out,
              w_ffn_up, w_ffn_dw, b_ffn_dw, w_ffn_down):
    bp = x_prompt.shape[0]
    xp, xs = x_prompt, x_sample
    conv_p, conv_s, gla_p, gla_s = [], [], [], []
    fk_p, fk_s, fv_p, fv_s, fl_p, fl_s = [], [], [], [], [], []
    ffn_p, ffn_s = [], []
    for i in range(DEPTH):
        m, j = i % N_MIXERS, i // N_MIXERS
        hp = rmsnorm(xp, norm_mix_g[i])
        hs = rmsnorm(xs, norm_mix_g[i])
        if m == 0:
            prm = (w_conv_in[j], w_conv_dw[j], b_conv_dw[j], conv_ln_g[j], conv_ln_b[j], w_conv_out[j])
            yp, st_p = conv_mixer(hp, jnp.zeros((bp, CONV_WIDTH - 1, CONV_DIM), hp.dtype), *prm)
            ys, st_s = conv_mixer(hs, state_conv[j], *prm)
            conv_p.append(st_p)
            conv_s.append(st_s)
        elif m == 1:
            prm = (w_gla_in[j], w_gla_gk2[j], b_gla_gk[j], gla_norm_g[j], w_gla_out[j])
            yp, st_p = gla_mixer(hp, jnp.zeros((bp, GLA_HEADS, GLA_HEAD_K, GLA_HEAD_V), state_gla.dtype), *prm)
            ys, st_s = gla_mixer(hs, state_gla[j], *prm)
            gla_p.append(st_p)
            gla_s.append(st_s)
        else:
            prm = (w_fox_in[j], b_fox_f[j], fox_qn_g[j], fox_kn_g[j], w_fox_out[j])
            empty_kv = jnp.zeros((bp, 0, FOX_HEADS, FOX_HEAD_DIM), cache_fox_k.dtype)
            empty_lf = jnp.zeros((bp, 0, FOX_HEADS), cache_fox_logf.dtype)
            yp, kp, vp, lp = fox_mixer(hp, empty_kv, empty_kv, empty_lf, *prm)
            ys, kn, vn, ln = fox_mixer(hs, paged_gather(cache_fox_k[j], page_table), paged_gather(cache_fox_v[j], page_table),
                                       paged_gather(cache_fox_logf[j], page_table), *prm)
            fk_p.append(kp)
            fk_s.append(kn)
            fv_p.append(vp)
            fv_s.append(vn)
            fl_p.append(lp)
            fl_s.append(ln)
        xp = xp + yp
        xs = xs + ys
        fprm = (w_ffn_up[i], w_ffn_dw[i], b_ffn_dw[i], w_ffn_down[i])
        yp, st_p = conv_ffn(rmsnorm(xp, norm_ffn_g[i]), jnp.zeros((bp, FFN_CONV_WIDTH - 1, 2 * FFN_DIM), xp.dtype), *fprm)
        ys, st_s = conv_ffn(rmsnorm(xs, norm_ffn_g[i]), state_ffn_conv[i], *fprm)
        ffn_p.append(st_p)
        ffn_s.append(st_s)
        xp = xp + yp
        xs = xs + ys
    return (xp, xs, jnp.stack(conv_p), jnp.stack(conv_s), jnp.stack(gla_p), jnp.stack(gla_s),
            jnp.stack(fk_p), jnp.stack(fk_s), jnp.stack(fv_p), jnp.stack(fv_s), jnp.stack(fl_p), jnp.stack(fl_s),
            jnp.stack(ffn_p), jnp.stack(ffn_s))
```

```python
import functools

import jax
import jax.numpy as jnp
from jax import lax
from jax.experimental import pallas as pl
from jax.experimental.pallas import tpu as pltpu

F32 = jnp.float32
BF16 = jnp.bfloat16
EPS = 1e-6

LANES = 128
SUBLANES = 8
VMEM_LIMIT_BYTES = 58 * 1024 * 1024

GLA_HEADS = 4
GLA_HEAD_K = 128
GLA_HEAD_V = 256
GLA_GATE_RANK = 16
GLA_GATE_NORM = 16.0
GLA_CHUNK = 64
FOX_HEADS = 16
FOX_HEAD_DIM = 64
CONV_WIDTH = 31
FFN_CONV_WIDTH = 3


def _params(*sem):
    return pltpu.CompilerParams(dimension_semantics=sem, vmem_limit_bytes=VMEM_LIMIT_BYTES)


def _resident(shape):
    nd = len(shape)
    return pl.BlockSpec(shape, lambda *_: (0,) * nd, pipeline_mode=pl.Buffered(1))


def _dot(a, b):
    return jnp.dot(a, b, preferred_element_type=F32)


def _dot_nt(a, b):
    return lax.dot_general(a, b, (((1,), (1,)), ((), ())), preferred_element_type=F32)


def _dot_tn(a, b):
    return lax.dot_general(a, b, (((0,), (0,)), ((), ())), preferred_element_type=F32)


def _split3(x):
    hi = x.astype(BF16).astype(F32)
    r = x - hi
    mid = r.astype(BF16).astype(F32)
    lo = (r - mid).astype(BF16).astype(F32)
    return hi, mid, lo


def _dot_exact_lhs01(t, x, mm):
    hi, mid, lo = _split3(x)
    return _dot(t, hi.astype(mm)) + _dot(t, mid.astype(mm)) + _dot(t, lo.astype(mm))


def _rms(x, g):
    return x * lax.rsqrt(jnp.mean(x * x, axis=-1, keepdims=True) + EPS) * g


def _log_sigmoid(z):
    return jnp.minimum(z, 0.0) - jnp.log1p(jnp.exp(-jnp.abs(z)))


def _mm_dtype(rows):
    return BF16 if rows % 16 == 0 else F32


def _norm_matmul_kernel(x_ref, g_ref, w_ref, o_ref, xn_sc):
    @pl.when(pl.program_id(1) == 0)
    def _():
        xn_sc[...] = _rms(x_ref[...], g_ref[...]).astype(BF16)

    o_ref[...] = _dot(xn_sc[...], w_ref[...])


def norm_matmul(x2, g, w_bf, tn):
    m, d = x2.shape
    n = w_bf.shape[1]
    tm = min(m, 1024)
    return pl.pallas_call(
        _norm_matmul_kernel,
        grid=(m // tm, n // tn),
        in_specs=[pl.BlockSpec((tm, d), lambda i, j: (i, 0)),
                  pl.BlockSpec((1, d), lambda i, j: (0, 0)),
                  pl.BlockSpec((d, tn), lambda i, j: (0, j))],
        out_specs=pl.BlockSpec((tm, tn), lambda i, j: (i, j)),
        out_shape=jax.ShapeDtypeStruct((m, n), F32),
        scratch_shapes=[pltpu.VMEM((tm, d), BF16)],
        compiler_params=_params("parallel", "arbitrary"),
        name="norm_matmul",
    )(x2, g.reshape(1, d), w_bf)


def _ffn_kernel(x_ref, g_ref, st_ref, wup_ref, wdw_ref, bdw_ref, wdn_ref, y_ref, nst_ref,
                u_sc, act_sc, *, bb, tl, f, cw):
    l = pl.program_id(1)
    m = bb * tl
    d = x_ref.shape[-1]
    pre = SUBLANES - (FFN_CONV_WIDTH - 1)

    @pl.when(l == 0)
    def _():
        u_sc[:, pre:SUBLANES, :] = st_ref[...]

    x = x_ref[...].reshape(m, d)
    hn = _rms(x, g_ref[...]).astype(BF16)
    u_sc[:, SUBLANES:, :] = _dot(hn, wup_ref[...]).reshape(bb, tl, 2 * f)

    def chunk(c, carry):
        ca = pl.multiple_of(c * cw, cw)
        cb = pl.multiple_of(f + c * cw, cw)

        def conv(col):
            acc = bdw_ref[:, pl.ds(col, cw)]
            for w in range(FFN_CONV_WIDTH):
                acc = acc + wdw_ref[w:w + 1, pl.ds(col, cw)] * u_sc[:, pre + w:pre + w + tl, pl.ds(col, cw)]
            return acc

        ya = conv(ca)
        yb = conv(cb)
        act = ya * jax.nn.sigmoid(ya) * yb
        act_sc[:, pl.ds(ca, cw)] = act.reshape(m, cw).astype(BF16)
        return carry

    lax.fori_loop(0, f // cw, chunk, 0)
    y_ref[...] = (x + _dot(act_sc[...], wdn_ref[...])).reshape(bb, tl, d)

    new = u_sc[:, tl + pre:tl + SUBLANES, :]

    @pl.when(l == pl.num_programs(1) - 1)
    def _():
        nst_ref[...] = new

    u_sc[:, pre:SUBLANES, :] = new


def conv_ffn(x, g, state, wup_bf, wdw, bdw, wdn_bf, *, bb, tl):
    b, l, d = x.shape
    f = wdn_bf.shape[0]
    cw = 256
    kern = functools.partial(_ffn_kernel, bb=bb, tl=tl, f=f, cw=cw)
    return pl.pallas_call(
        kern,
        grid=(b // bb, l // tl),
        in_specs=[pl.BlockSpec((bb, tl, d), lambda i, j: (i, j, 0)),
                  _resident((1, d)),
                  pl.BlockSpec((bb, FFN_CONV_WIDTH - 1, 2 * f), lambda i, j: (i, 0, 0)),
                  _resident((d, 2 * f)),
                  _resident((FFN_CONV_WIDTH, 2 * f)),
                  _resident((1, 2 * f)),
                  _resident((f, d))],
        out_specs=[pl.BlockSpec((bb, tl, d), lambda i, j: (i, j, 0)),
                   pl.BlockSpec((bb, FFN_CONV_WIDTH - 1, 2 * f), lambda i, j: (i, 0, 0))],
        out_shape=[jax.ShapeDtypeStruct((b, l, d), F32),
                   jax.ShapeDtypeStruct((b, FFN_CONV_WIDTH - 1, 2 * f), F32)],
        scratch_shapes=[pltpu.VMEM((bb, SUBLANES + tl, 2 * f), F32),
                        pltpu.VMEM((bb * tl, f), BF16)],
        compiler_params=_params("parallel", "arbitrary"),
        name="conv_ffn",
    )(x, g.reshape(1, d), state, wup_bf, wdw, bdw.reshape(1, 2 * f), wdn_bf)


def _conv_mixer_kernel(x_ref, g_ref, st_ref, win_ref, wdw_ref, bdw_ref, lng_ref, lnb_ref, wout_ref,
                       y_ref, nst_ref, buf_sc, act_sc, *, bb, tl, bbk, rb):
    l = pl.program_id(1)
    m = bb * tl
    d = x_ref.shape[-1]
    c = wout_ref.shape[0]
    hist = CONV_WIDTH - 1
    head = 4 * SUBLANES
    off = head - hist

    @pl.when(l == 0)
    def _():
        buf_sc[:, off:head, :] = st_ref[...]

    x = x_ref[...].reshape(m, d)
    hn = _rms(x, g_ref[...]).astype(BF16)
    ag = _dot(hn, win_ref[...])
    u = ag[:, :c] * jax.nn.sigmoid(ag[:, c:])
    buf_sc[:, head:, :] = u.reshape(bb, tl, c)

    for b0 in range(0, bb, bbk):
        for r0 in range(0, tl, rb):
            acc = jnp.broadcast_to(bdw_ref[...].reshape(1, 1, c), (bbk, rb, c))
            for w in range(CONV_WIDTH):
                acc = acc + wdw_ref[w:w + 1, :].reshape(1, 1, c) * buf_sc[b0:b0 + bbk, r0 + off + w:r0 + off + w + rb, :]
            mu = jnp.mean(acc, axis=-1, keepdims=True)
            xc = acc - mu
            var = jnp.mean(xc * xc, axis=-1, keepdims=True)
            y = xc * lax.rsqrt(var + EPS) * lng_ref[...].reshape(1, 1, c) + lnb_ref[...].reshape(1, 1, c)
            act_sc[b0:b0 + bbk, r0:r0 + rb, :] = (y * jax.nn.sigmoid(y)).astype(act_sc.dtype)
    y_ref[...] = (x + _dot(act_sc[...].reshape(m, c).astype(BF16), wout_ref[...])).reshape(bb, tl, d)

    new = buf_sc[:, tl + off:tl + head, :]

    @pl.when(l == pl.num_programs(1) - 1)
    def _():
        nst_ref[...] = new

    buf_sc[:, off:head, :] = new


def conv_mixer(x, g, state, win_bf, wdw, bdw, lng, lnb, wout_bf, *, bb, tl, bbk, rb):
    b, l, d = x.shape
    c = wout_bf.shape[0]
    hist = CONV_WIDTH - 1
    kern = functools.partial(_conv_mixer_kernel, bb=bb, tl=tl, bbk=bbk, rb=rb)
    return pl.pallas_call(
        kern,
        grid=(b // bb, l // tl),
        in_specs=[pl.BlockSpec((bb, tl, d), lambda i, j: (i, j, 0)),
                  _resident((1, d)),
                  pl.BlockSpec((bb, hist, c), lambda i, j: (i, 0, 0)),
                  _resident((d, 2 * c)),
                  _resident((CONV_WIDTH, c)),
                  _resident((1, c)),
                  _resident((1, c)),
                  _resident((1, c)),
                  _resident((c, d))],
        out_specs=[pl.BlockSpec((bb, tl, d), lambda i, j: (i, j, 0)),
                   pl.BlockSpec((bb, hist, c), lambda i, j: (i, 0, 0))],
        out_shape=[jax.ShapeDtypeStruct((b, l, d), F32),
                   jax.ShapeDtypeStruct((b, hist, c), F32)],
        scratch_shapes=[pltpu.VMEM((bb, 4 * SUBLANES + tl, c), F32),
                        pltpu.VMEM((bb, tl, c), _mm_dtype(rb))],
        compiler_params=_params("parallel", "arbitrary"),
        name="conv_mixer",
    )(x, g.reshape(1, d), state, win_bf, wdw, bdw.reshape(1, c), lng.reshape(1, c), lnb.reshape(1, c), wout_bf)


def _gla_gate_kernel(g1_ref, w2_ref, b_ref, o_ref):
    z = _dot(g1_ref[...].astype(BF16), w2_ref[...]) + b_ref[...]
    o_ref[...] = _log_sigmoid(z) * (1.0 / GLA_GATE_NORM)


def gla_gate(h2, w2_pad_bf, b):
    m = h2.shape[0]
    dk = w2_pad_bf.shape[1]
    tm = min(m, 1024)
    g1_block = (2 * dk + 2 * GLA_HEADS * GLA_HEAD_V) // LANES
    return pl.pallas_call(
        _gla_gate_kernel,
        grid=(m // tm,),
        in_specs=[pl.BlockSpec((tm, LANES), lambda i: (i, g1_block)),
                  _resident((LANES, dk)),
                  _resident((1, dk))],
        out_specs=pl.BlockSpec((tm, dk), lambda i: (i, 0)),
        out_shape=jax.ShapeDtypeStruct((m, dk), F32),
        compiler_params=_params("parallel"),
        name="gla_gate",
    )(h2, w2_pad_bf, b.reshape(1, dk))


def _gla_rec_kernel(q_ref, k_ref, v_ref, g_ref, s0_ref, o_ref, s_ref, st_sc, *, tl, chunk):
    l = pl.program_id(2)
    mm = _mm_dtype(chunk)

    @pl.when(l == 0)
    def _():
        st_sc[...] = s0_ref[0, 0].T

    q = q_ref[0] * (GLA_HEAD_K ** -0.5)
    k = k_ref[0]
    v = v_ref[0]
    g = g_ref[0]

    row = lax.broadcasted_iota(jnp.int32, (tl, tl), 0)
    col = lax.broadcasted_iota(jnp.int32, (tl, tl), 1)
    tril = ((row // chunk == col // chunk) & (col <= row)).astype(mm)
    gcum = _dot_exact_lhs01(tril, g, mm)

    crow = lax.broadcasted_iota(jnp.int32, (chunk, chunk), 0)
    ccol = lax.broadcasted_iota(jnp.int32, (chunk, chunk), 1)
    causal = ccol <= crow

    st = st_sc[...]
    outs = []
    for c in range(tl // chunk):
        sl = slice(c * chunk, (c + 1) * chunk)
        gc = gcum[sl]
        gl = gc[chunk - 1:chunk]
        qe = (q[sl] * jnp.exp(gc)).astype(mm)
        ke = (k[sl] * jnp.exp(-gc)).astype(mm)
        kd = (k[sl] * jnp.exp(gl - gc)).astype(mm)
        vc = v[sl].astype(mm)
        scores = jnp.where(causal, _dot_nt(qe, ke), 0.0)
        outs.append(_dot_nt(qe, st.astype(mm)) + _dot(scores.astype(mm), vc))
        st = st * jnp.exp(gl) + _dot_tn(vc, kd)
    o_ref[0] = outs[0] if len(outs) == 1 else jnp.concatenate(outs, axis=0)
    st_sc[...] = st

    @pl.when(l == pl.num_programs(2) - 1)
    def _():
        s_ref[0, 0] = st.T


def gla_recurrence(h3, gk3, s0, *, tl, chunk):
    b, l, _ = h3.shape
    hk, hv, nh = GLA_HEAD_K, GLA_HEAD_V, GLA_HEADS
    k_off = (nh * hk) // hk
    v_off = (2 * nh * hk) // hv
    kern = functools.partial(_gla_rec_kernel, tl=tl, chunk=chunk)
    return pl.pallas_call(
        kern,
        grid=(b, nh, l // tl),
        in_specs=[pl.BlockSpec((1, tl, hk), lambda i, h, j: (i, j, h)),
                  pl.BlockSpec((1, tl, hk), lambda i, h, j: (i, j, k_off + h)),
                  pl.BlockSpec((1, tl, hv), lambda i, h, j: (i, j, v_off + h)),
                  pl.BlockSpec((1, tl, hk), lambda i, h, j: (i, j, h)),
                  pl.BlockSpec((1, 1, hk, hv), lambda i, h, j: (i, h, 0, 0))],
        out_specs=[pl.BlockSpec((1, tl, hv), lambda i, h, j: (i, j, h)),
                   pl.BlockSpec((1, 1, hk, hv), lambda i, h, j: (i, h, 0, 0))],
        out_shape=[jax.ShapeDtypeStruct((b, l, nh * hv), F32),
                   jax.ShapeDtypeStruct((b, nh, hk, hv), F32)],
        scratch_shapes=[pltpu.VMEM((hv, hk), F32)],
        compiler_params=_params("parallel", "parallel", "arbitrary"),
        name="gla_recurrence",
    )(h3, h3, h3, gk3, s0)


def _gla_out_kernel(x_ref, o_ref, r_ref, gn_ref, w_ref, y_ref):
    o = o_ref[...]
    gn = gn_ref[...]
    parts = []
    for h in range(GLA_HEADS):
        parts.append(_rms(o[:, h * GLA_HEAD_V:(h + 1) * GLA_HEAD_V], gn))
    on = jnp.concatenate(parts, axis=-1)
    r = r_ref[...]
    gated = (on * (r * jax.nn.sigmoid(r))).astype(BF16)
    y_ref[...] = x_ref[...] + _dot(gated, w_ref[...])


def gla_out(x2, o2, h2, gn, wout_bf):
    m, d = x2.shape
    dv = o2.shape[1]
    tm = min(m, 512)
    r_block = (2 * GLA_HEADS * GLA_HEAD_K + dv) // dv
    return pl.pallas_call(
        _gla_out_kernel,
        grid=(m // tm,),
        in_specs=[pl.BlockSpec((tm, d), lambda i: (i, 0)),
                  pl.BlockSpec((tm, dv), lambda i: (i, 0)),
                  pl.BlockSpec((tm, dv), lambda i: (i, r_block)),
                  _resident((1, GLA_HEAD_V)),
                  _resident((dv, d))],
        out_specs=pl.BlockSpec((tm, d), lambda i: (i, 0)),
        out_shape=jax.ShapeDtypeStruct((m, d), F32),
        compiler_params=_params("parallel"),
        name="gla_out",
    )(x2, o2, h2, gn.reshape(1, GLA_HEAD_V), wout_bf)


def _head_norm(x, g2):
    lo = lax.broadcasted_iota(jnp.int32, (1, LANES), 1) < FOX_HEAD_DIM
    parts = []
    for j in range(x.shape[-1] // LANES):
        xb = x[:, j * LANES:(j + 1) * LANES]
        sq = xb * xb
        s_lo = jnp.sum(jnp.where(lo, sq, 0.0), axis=-1, keepdims=True)
        s_hi = jnp.sum(jnp.where(lo, 0.0, sq), axis=-1, keepdims=True)
        ms = jnp.where(lo, s_lo, s_hi) * (1.0 / FOX_HEAD_DIM)
        parts.append(xb * lax.rsqrt(ms + EPS) * g2)
    return jnp.concatenate(parts, axis=-1)


def _fox_prep_kernel(q_ref, k_ref, v_ref, fl_ref, qg_ref, kg_ref, bf_ref,
                     qa_ref, kn_ref, ka_ref, vo_ref, va_ref, lf_ref, c_ref, ct_ref, carry_sc, *, tl):
    l = pl.program_id(1)
    mm = _mm_dtype(tl)

    @pl.when(l == 0)
    def _():
        carry_sc[...] = jnp.zeros_like(carry_sc)

    qn = _head_norm(q_ref[0], qg_ref[...]) * (FOX_HEAD_DIM ** -0.5)
    kn = _head_norm(k_ref[0], kg_ref[...])
    v = v_ref[0]
    qa_ref[0] = qn.astype(qa_ref.dtype)
    kn_ref[0] = kn
    ka_ref[0] = kn.astype(ka_ref.dtype)
    vo_ref[0] = v
    va_ref[0] = v.astype(va_ref.dtype)

    lf = _log_sigmoid(fl_ref[0] + bf_ref[...])
    lf_ref[0] = lf[:, :FOX_HEADS]
    row = lax.broadcasted_iota(jnp.int32, (tl, tl), 0)
    col = lax.broadcasted_iota(jnp.int32, (tl, tl), 1)
    tril = (col <= row).astype(mm)
    c = _dot_exact_lhs01(tril, lf, mm) + carry_sc[...]
    carry_sc[...] = c[tl - 1:tl]
    c_ref[0] = c
    if tl % LANES == 0:
        ct_ref[0] = c.T
    else:
        ct_ref[0] = jnp.zeros_like(ct_ref[0])


def fox_prep(h3, qg, kg, bf, *, tl, att_dtype):
    b, l, _ = h3.shape
    d = FOX_HEADS * FOX_HEAD_DIM
    fl_block = (4 * d) // LANES
    qg2 = jnp.tile(qg, 2).reshape(1, LANES)
    kg2 = jnp.tile(kg, 2).reshape(1, LANES)
    bf_pad = jnp.pad(bf, (0, LANES - FOX_HEADS)).reshape(1, LANES)
    ct_cols = tl if tl % LANES == 0 else LANES
    row3 = lambda i, j: (i, j, 0)
    kern = functools.partial(_fox_prep_kernel, tl=tl)
    return pl.pallas_call(
        kern,
        grid=(b, l // tl),
        in_specs=[pl.BlockSpec((1, tl, d), lambda i, j: (i, j, 0)),
                  pl.BlockSpec((1, tl, d), lambda i, j: (i, j, 1)),
                  pl.BlockSpec((1, tl, d), lambda i, j: (i, j, 2)),
                  pl.BlockSpec((1, tl, LANES), lambda i, j: (i, j, fl_block)),
                  _resident((1, LANES)), _resident((1, LANES)), _resident((1, LANES))],
        out_specs=[pl.BlockSpec((1, tl, d), row3),
                   pl.BlockSpec((1, tl, d), row3),
                   pl.BlockSpec((1, tl, d), row3),
                   pl.BlockSpec((1, tl, d), row3),
                   pl.BlockSpec((1, tl, d), row3),
                   pl.BlockSpec((1, tl, FOX_HEADS), row3),
                   pl.BlockSpec((1, tl, LANES), row3),
                   pl.BlockSpec((1, LANES, ct_cols), lambda i, j: (i, 0, j))],
        out_shape=[jax.ShapeDtypeStruct((b, l, d), att_dtype),
                   jax.ShapeDtypeStruct((b, l, d), F32),
                   jax.ShapeDtypeStruct((b, l, d), att_dtype),
                   jax.ShapeDtypeStruct((b, l, d), F32),
                   jax.ShapeDtypeStruct((b, l, d), att_dtype),
                   jax.ShapeDtypeStruct((b, l, FOX_HEADS), F32),
                   jax.ShapeDtypeStruct((b, l, LANES), F32),
                   jax.ShapeDtypeStruct((b, LANES, (l // tl) * ct_cols), F32)],
        scratch_shapes=[pltpu.VMEM((1, LANES), F32)],
        compiler_params=_params("parallel", "arbitrary"),
        name="fox_prep",
    )(h3, h3, h3, h3, qg2, kg2, bf_pad)


def _flash_kernel(q_ref, k_ref, v_ref, cq_ref, ck_ref, o_ref, m_sc, l_sc, acc_sc, *, tq, tk):
    qi = pl.program_id(1)
    ki = pl.program_id(2)
    last_k = ((qi + 1) * tq - 1) // tk
    lo = lax.broadcasted_iota(jnp.int32, (1, LANES), 1) < FOX_HEAD_DIM

    @pl.when(ki == 0)
    def _():
        m_sc[...] = jnp.full_like(m_sc, -jnp.inf)
        l_sc[...] = jnp.zeros_like(l_sc)
        acc_sc[...] = jnp.zeros_like(acc_sc)

    @pl.when(ki <= last_k)
    def _():
        qpos = qi * tq + lax.broadcasted_iota(jnp.int32, (tq, tk), 0)
        kpos = ki * tk + lax.broadcasted_iota(jnp.int32, (tq, tk), 1)
        visible = kpos <= qpos
        for pair in range(FOX_HEADS // 2):
            cols = slice(pair * LANES, (pair + 1) * LANES)
            q2 = q_ref[0, :, cols]
            k2 = k_ref[0, :, cols]
            v2 = v_ref[0, :, cols]
            alphas = []
            pvs = []
            for half in range(2):
                h = 2 * pair + half
                keep = lo if half == 0 else jnp.logical_not(lo)
                qm = jnp.where(keep, q2, jnp.zeros_like(q2))
                s = _dot_nt(qm, k2)
                s = s + (cq_ref[0, :, h:h + 1] - ck_ref[0, h:h + 1, :])
                s = jnp.where(visible, s, -jnp.inf)
                m_prev = m_sc[h]
                m_new = jnp.maximum(m_prev, jnp.max(s, axis=-1, keepdims=True))
                alpha = jnp.exp(m_prev - m_new)
                p = jnp.exp(s - m_new[:, :1])
                l_sc[h] = alpha * l_sc[h] + jnp.sum(p, axis=-1, keepdims=True)
                m_sc[h] = m_new
                alphas.append(alpha)
                pvs.append(_dot(p.astype(BF16), v2))
            acc_sc[:, cols] = acc_sc[:, cols] * jnp.where(lo, alphas[0], alphas[1]) + jnp.where(lo, pvs[0], pvs[1])

    @pl.when(ki == last_k)
    def _():
        for pair in range(FOX_HEADS // 2):
            cols = slice(pair * LANES, (pair + 1) * LANES)
            l2 = jnp.where(lo, l_sc[2 * pair], l_sc[2 * pair + 1])
            o_ref[0, :, cols] = acc_sc[:, cols] / l2


def fox_flash(qb, kb, vb, c3, ct3, *, tq, tk):
    b, l, d = qb.shape

    def kv_map(i, qi, ki):
        return (i, jnp.minimum(ki, ((qi + 1) * tq - 1) // tk), 0)

    def ck_map(i, qi, ki):
        return (i, 0, jnp.minimum(ki, ((qi + 1) * tq - 1) // tk))

    kern = functools.partial(_flash_kernel, tq=tq, tk=tk)
    return pl.pallas_call(
        kern,
        grid=(b, l // tq, l // tk),
        in_specs=[pl.BlockSpec((1, tq, d), lambda i, qi, ki: (i, qi, 0)),
                  pl.BlockSpec((1, tk, d), kv_map),
                  pl.BlockSpec((1, tk, d), kv_map),
                  pl.BlockSpec((1, tq, LANES), lambda i, qi, ki: (i, qi, 0)),
                  pl.BlockSpec((1, FOX_HEADS, tk), ck_map)],
        out_specs=pl.BlockSpec((1, tq, d), lambda i, qi, ki: (i, qi, 0)),
        out_shape=jax.ShapeDtypeStruct((b, l, d), F32),
        scratch_shapes=[pltpu.VMEM((FOX_HEADS, tq, LANES), F32),
                        pltpu.VMEM((FOX_HEADS, tq, LANES), F32),
                        pltpu.VMEM((tq, d), F32)],
        compiler_params=_params("parallel", "parallel", "arbitrary"),
        name="fox_flash",
    )(qb, kb, vb, c3, ct3)


def _decode_kernel(pt_ref, q_ref, kn_ref, vn_ref, lfn_ref, kp_ref, vp_ref, lfp_ref, o_ref,
                   qbd_sc, m_sc, l_sc, acc_sc, cq_sc, carry_sc, *, lq, page):
    del pt_ref
    p = pl.program_id(1)
    rows = FOX_HEADS * lq
    d = FOX_HEADS * FOX_HEAD_DIM

    uj = lax.broadcasted_iota(jnp.int32, (page, 2 * page), 0)
    us = lax.broadcasted_iota(jnp.int32, (page, 2 * page), 1)
    u2 = ((uj > us) | (us >= page)).astype(BF16)
    er = lax.broadcasted_iota(jnp.int32, (rows, FOX_HEADS), 0)
    ec = lax.broadcasted_iota(jnp.int32, (rows, FOX_HEADS), 1)
    expand = (er // lq == ec).astype(BF16)

    def gate_sums(lf_tile, keep):
        out = jnp.zeros((rows, 2 * page), F32)
        for piece in _split3(lf_tile):
            e = _dot_nt(expand, piece.astype(BF16))
            if keep is not None:
                e = jnp.where(keep, e, 0.0)
            out = out + _dot(e.astype(BF16), u2)
        return out

    def online_update(s, v_bf, first):
        m_cur = jnp.max(s, axis=-1, keepdims=True)
        if first:
            m_new = jnp.broadcast_to(m_cur, (rows, LANES))
            pexp = jnp.exp(s - m_cur)
            l_sc[...] = jnp.broadcast_to(jnp.sum(pexp, axis=-1, keepdims=True), (rows, LANES))
            acc_sc[...] = _dot(pexp.astype(BF16), v_bf)
        else:
            m_prev = m_sc[...]
            m_new = jnp.maximum(m_prev, m_cur)
            alpha = jnp.exp(m_prev - m_new)
            pexp = jnp.exp(s - m_new[:, :1])
            l_sc[...] = alpha * l_sc[...] + jnp.sum(pexp, axis=-1, keepdims=True)
            pv = _dot(pexp.astype(BF16), v_bf)
            for j in range(d // LANES):
                cols = slice(j * LANES, (j + 1) * LANES)
                acc_sc[:, cols] = acc_sc[:, cols] * alpha + pv[:, cols]
        m_sc[...] = m_new

    @pl.when(p == 0)
    def _():
        q = q_ref[0].astype(F32)
        qt = jnp.concatenate([q] * FOX_HEADS, axis=0)
        rr = lax.broadcasted_iota(jnp.int32, (rows, d), 0)
        cc = lax.broadcasted_iota(jnp.int32, (rows, d), 1)
        qbd_sc[...] = jnp.where(rr // lq == cc // FOX_HEAD_DIM, qt, 0.0).astype(BF16)

        pad = page - lq
        kn = jnp.concatenate([kn_ref[0].astype(F32), jnp.zeros((pad, d), F32)], axis=0).astype(BF16)
        vn = jnp.concatenate([vn_ref[0].astype(F32), jnp.zeros((pad, d), F32)], axis=0).astype(BF16)
        lfn = jnp.concatenate([lfn_ref[0], jnp.zeros((pad, FOX_HEADS), F32)], axis=0)
        t = lax.broadcasted_iota(jnp.int32, (rows, page), 0) % lq
        j = lax.broadcasted_iota(jnp.int32, (rows, page), 1)
        sums = gate_sums(lfn, j <= t)
        cq_sc[...] = sums[:, page:]
        s = _dot_nt(qbd_sc[...], kn) + sums[:, :page]
        s = jnp.where(j <= t, s, -jnp.inf)
        online_update(s, vn, True)
        carry_sc[...] = jnp.zeros_like(carry_sc)

    sums = gate_sums(lfp_ref[0], None)
    bias = sums[:, :page] + carry_sc[...] + cq_sc[...]
    carry_sc[...] = carry_sc[...] + sums[:, page:]
    s = _dot_nt(qbd_sc[...], kp_ref[0].astype(BF16)) + bias
    online_update(s, vp_ref[0].astype(BF16), False)

    @pl.when(p == pl.num_programs(1) - 1)
    def _():
        rr = lax.broadcasted_iota(jnp.int32, (rows, d), 0)
        cc = lax.broadcasted_iota(jnp.int32, (rows, d), 1)
        inv = 1.0 / l_sc[...]
        parts = [acc_sc[:, j * LANES:(j + 1) * LANES] * inv for j in range(d // LANES)]
        on = jnp.where(rr // lq == cc // FOX_HEAD_DIM, jnp.concatenate(parts, axis=-1), 0.0)
        o_ref[0] = jnp.sum(on.reshape(FOX_HEADS, lq, d), axis=0)


def fox_decode(q3, kn3, vn3, lfn3, kpool, vpool, lfpool, page_table):
    b, lq, d = q3.shape
    n_pages = page_table.shape[1]
    page = kpool.shape[1]
    rows = FOX_HEADS * lq
    assert page == LANES and rows == LANES

    def new_map(i, p, pt):
        return (i, 0, 0)

    def page_map(i, p, pt):
        return (pt[i, n_pages - 1 - p], 0, 0)

    kern = functools.partial(_decode_kernel, lq=lq, page=page)
    grid_spec = pltpu.PrefetchScalarGridSpec(
        num_scalar_prefetch=1,
        grid=(b, n_pages),
        in_specs=[pl.BlockSpec((1, lq, d), new_map),
                  pl.BlockSpec((1, lq, d), new_map),
                  pl.BlockSpec((1, lq, d), new_map),
                  pl.BlockSpec((1, lq, FOX_HEADS), new_map),
                  pl.BlockSpec((1, page, d), page_map),
                  pl.BlockSpec((1, page, d), page_map),
                  pl.BlockSpec((1, page, FOX_HEADS), page_map)],
        out_specs=pl.BlockSpec((1, lq, d), new_map),
        scratch_shapes=[pltpu.VMEM((rows, d), BF16),
                        pltpu.VMEM((rows, LANES), F32),
                        pltpu.VMEM((rows, LANES), F32),
                        pltpu.VMEM((rows, d), F32),
                        pltpu.VMEM((rows, LANES), F32),
                        pltpu.VMEM((rows, LANES), F32)])
    return pl.pallas_call(
        kern,
        grid_spec=grid_spec,
        out_shape=jax.ShapeDtypeStruct((b, lq, d), F32),
        compiler_params=_params("parallel", "arbitrary"),
        name="fox_decode",
    )(page_table, q3, kn3, vn3, lfn3, kpool, vpool, lfpool)


def _fox_out_kernel(x_ref, o_ref, og_ref, w_ref, y_ref):
    gated = (o_ref[...] * jax.nn.sigmoid(og_ref[...])).astype(BF16)
    y_ref[...] = x_ref[...] + _dot(gated, w_ref[...])


def fox_out(x2, o2, h2, wout_bf):
    m, d = x2.shape
    tm = min(m, 512)
    return pl.pallas_call(
        _fox_out_kernel,
        grid=(m // tm,),
        in_specs=[pl.BlockSpec((tm, d), lambda i: (i, 0)),
                  pl.BlockSpec((tm, d), lambda i: (i, 0)),
                  pl.BlockSpec((tm, d), lambda i: (i, 3)),
                  _resident((d, d))],
        out_specs=pl.BlockSpec((tm, d), lambda i: (i, 0)),
        out_shape=jax.ShapeDtypeStruct((m, d), F32),
        compiler_params=_params("parallel"),
        name="fox_out",
    )(x2, o2, h2, wout_bf)


def _tiles(x):
    b, l, _ = x.shape
    if l >= 512:
        return 1, 512
    return b, l


def _conv_layer(x, g, state, prm):
    bb, tl = _tiles(x)
    tl = min(tl, 256)
    bbk, rb = (1, 16) if tl >= 16 else (16 // tl, tl)
    return conv_mixer(x, g, state, *prm, bb=bb, tl=tl, bbk=bbk, rb=rb)


def _ffn_layer(x, g, state, prm):
    bb, tl = _tiles(x)
    return conv_ffn(x, g, state, *prm, bb=bb, tl=tl)


def _gla_layer(x, g, s0, prm):
    win_bf, w2_bf, b_gk, gn, wout_bf = prm
    b, l, d = x.shape
    x2 = x.reshape(b * l, d)
    h2 = norm_matmul(x2, g, win_bf, 640)
    gk2 = gla_gate(h2, w2_bf, b_gk)
    chunk = GLA_CHUNK if l % GLA_CHUNK == 0 else l
    tl = 4 * chunk if l % (4 * chunk) == 0 else chunk
    o3, s = gla_recurrence(h2.reshape(b, l, -1), gk2.reshape(b, l, -1), s0, tl=tl, chunk=chunk)
    y2 = gla_out(x2, o3.reshape(b * l, -1), h2, gn, wout_bf)
    return y2.reshape(b, l, d), s


def _fox_layer(x, g, cache, prm):
    win_bf, b_f, qg, kg, wout_bf = prm
    b, l, d = x.shape
    x2 = x.reshape(b * l, d)
    h2 = norm_matmul(x2, g, win_bf, 1408)
    h3 = h2.reshape(b, l, -1)
    if cache is None:
        qa, kn, ka, vo, va, lf, c3, ct3 = fox_prep(h3, qg, kg, b_f, tl=512, att_dtype=BF16)
        o3 = fox_flash(qa, ka, va, c3, ct3, tq=256, tk=512)
    else:
        kpool, vpool, lfpool, page_table = cache
        qa, kn, ka, vo, va, lf, c3, ct3 = fox_prep(h3, qg, kg, b_f, tl=l, att_dtype=F32)
        o3 = fox_decode(qa, kn, vo, lf, kpool, vpool, lfpool, page_table)
    y2 = fox_out(x2, o3.reshape(b * l, d), h2, wout_bf)
    shp = (b, l, FOX_HEADS, FOX_HEAD_DIM)
    return y2.reshape(b, l, d), kn.reshape(shp), vo.reshape(shp), lf


def _pad_cols(w, n):
    return jnp.pad(w, ((0, 0), (0, n - w.shape[1])))


def kernel(x_prompt, x_sample, state_conv, state_gla, cache_fox_k, cache_fox_v, cache_fox_logf, state_ffn_conv, page_table,
           norm_mix_g, norm_ffn_g, w_conv_in, w_conv_dw, b_conv_dw, conv_ln_g, conv_ln_b, w_conv_out,
           w_gla_in, w_gla_gk2, b_gla_gk, gla_norm_g, w_gla_out,
           w_fox_in, b_fox_f, fox_qn_g, fox_kn_g, w_fox_out,
           w_ffn_up, w_ffn_dw, b_ffn_dw, w_ffn_down):
    depth = norm_mix_g.shape[0]
    bp = x_prompt.shape[0]
    d = x_prompt.shape[-1]
    xp, xs = x_prompt, x_sample
    conv_p, conv_s, gla_p, gla_s = [], [], [], []
    fk_p, fk_s, fv_p, fv_s, fl_p, fl_s = [], [], [], [], [], []
    ffn_p, ffn_s = [], []
    for i in range(depth):
        m, j = i % 3, i // 3
        if m == 0:
            prm = (w_conv_in[j].astype(BF16), w_conv_dw[j], b_conv_dw[j], conv_ln_g[j], conv_ln_b[j],
                   w_conv_out[j].astype(BF16))
            zero = jnp.zeros((bp,) + state_conv.shape[2:], F32)
            xp, st_p = _conv_layer(xp, norm_mix_g[i], zero, prm)
            xs, st_s = _conv_layer(xs, norm_mix_g[i], state_conv[j], prm)
            conv_p.append(st_p)
            conv_s.append(st_s)
        elif m == 1:
            n_in = w_gla_in.shape[-1]
            n_pad = -(-n_in // (5 * LANES)) * (5 * LANES)
            w2 = jnp.pad(w_gla_gk2[j], ((0, LANES - GLA_GATE_RANK), (0, 0))).astype(BF16)
            prm = (_pad_cols(w_gla_in[j], n_pad).astype(BF16), w2, b_gla_gk[j], gla_norm_g[j],
                   w_gla_out[j].astype(BF16))
            zero = jnp.zeros((bp,) + state_gla.shape[2:], F32)
            xp, st_p = _gla_layer(xp, norm_mix_g[i], zero, prm)
            xs, st_s = _gla_layer(xs, norm_mix_g[i], state_gla[j], prm)
            gla_p.append(st_p)
            gla_s.append(st_s)
        else:
            n_pad = 4 * d + LANES
            prm = (_pad_cols(w_fox_in[j], n_pad).astype(BF16), b_fox_f[j], fox_qn_g[j], fox_kn_g[j],
                   w_fox_out[j].astype(BF16))
            n_pool, page = cache_fox_k.shape[1], cache_fox_k.shape[2]
            cache = (cache_fox_k[j].reshape(n_pool, page, d), cache_fox_v[j].reshape(n_pool, page, d),
                     cache_fox_logf[j], page_table)
            xp, kp, vp, lp = _fox_layer(xp, norm_mix_g[i], None, prm)
            xs, kn, vn, ln = _fox_layer(xs, norm_mix_g[i], cache, prm)
            fk_p.append(kp)
            fk_s.append(kn)
            fv_p.append(vp)
            fv_s.append(vn)
            fl_p.append(lp)
            fl_s.append(ln)
        fprm = (w_ffn_up[i].astype(BF16), w_ffn_dw[i], b_ffn_dw[i], w_ffn_down[i].astype(BF16))
        zero = jnp.zeros((bp,) + state_ffn_conv.shape[2:], F32)
        xp, st_p = _ffn_layer(xp, norm_ffn_g[i], zero, fprm)
        xs, st_s = _ffn_layer(xs, norm_ffn_g[i], state_ffn_conv[i], fprm)
        ffn_p.append(st_p)
        ffn_s.append(st_s)
    return (xp, xs, jnp.stack(conv_p), jnp.stack(conv_s), jnp.stack(gla_p), jnp.stack(gla_s),
            jnp.stack(fk_p), jnp.stack(fk_s), jnp.stack(fv_p), jnp.stack(fv_s), jnp.stack(fl_p), jnp.stack(fl_s),
            jnp.stack(ffn_p), jnp.stack(ffn_s))
```

```python
import functools

import jax
import jax.numpy as jnp
import numpy as np
from jax import lax
from jax.experimental import pallas as pl
from jax.experimental.pallas import tpu as pltpu

F32 = jnp.float32
BF16 = jnp.bfloat16
EPS = 1e-6
LOG2E = 1.4426950408889634

LANES = 128
SUBLANES = 8
VMEM_LIMIT_BYTES = 58 * 1024 * 1024

GLA_HEADS = 4
GLA_HEAD_K = 128
GLA_HEAD_V = 256
GLA_GATE_RANK = 16
GLA_GATE_NORM = 16.0
GLA_CHUNK = 64
FOX_HEADS = 16
FOX_HEAD_DIM = 64
CONV_WIDTH = 31
FFN_CONV_WIDTH = 3
FFN_CHUNK = 256


def _params(*sem):
    return pltpu.CompilerParams(dimension_semantics=sem, vmem_limit_bytes=VMEM_LIMIT_BYTES)


def _resident(shape):
    nd = len(shape)
    return pl.BlockSpec(shape, lambda *_: (0,) * nd, pipeline_mode=pl.Buffered(1))


def _dot(a, b):
    return jnp.dot(a, b, preferred_element_type=F32)


def _dot_nt(a, b):
    return lax.dot_general(a, b, (((1,), (1,)), ((), ())), preferred_element_type=F32)


def _dot_tn(a, b):
    return lax.dot_general(a, b, (((0,), (0,)), ((), ())), preferred_element_type=F32)


def _split3(x):
    hi = x.astype(BF16).astype(F32)
    r = x - hi
    mid = r.astype(BF16).astype(F32)
    lo = (r - mid).astype(BF16).astype(F32)
    return hi, mid, lo


def _dot_exact_lhs01(t, x, mm):
    hi, mid, lo = _split3(x)
    return _dot(t, hi.astype(mm)) + _dot(t, mid.astype(mm)) + _dot(t, lo.astype(mm))


def _rms(x, g):
    return x * lax.rsqrt(jnp.mean(x * x, axis=-1, keepdims=True) + EPS) * g


def _log_sigmoid(z):
    return jnp.minimum(z, 0.0) - jnp.log1p(jnp.exp(-jnp.abs(z)))


def _mm_dtype(rows):
    return BF16 if rows % 16 == 0 else F32


def _norm_matmul_kernel(x_ref, g_ref, w_ref, o_ref, xn_sc):
    @pl.when(pl.program_id(1) == 0)
    def _():
        xn_sc[...] = _rms(x_ref[...], g_ref[...]).astype(BF16)

    o_ref[...] = _dot(xn_sc[...], w_ref[...])


def norm_matmul(x2, g, w_bf, tn):
    m, d = x2.shape
    n = w_bf.shape[1]
    tm = min(m, 1024)
    return pl.pallas_call(
        _norm_matmul_kernel,
        grid=(m // tm, n // tn),
        in_specs=[pl.BlockSpec((tm, d), lambda i, j: (i, 0)),
                  pl.BlockSpec((1, d), lambda i, j: (0, 0)),
                  pl.BlockSpec((d, tn), lambda i, j: (0, j))],
        out_specs=pl.BlockSpec((tm, tn), lambda i, j: (i, j)),
        out_shape=jax.ShapeDtypeStruct((m, n), F32),
        scratch_shapes=[pltpu.VMEM((tm, d), BF16)],
        compiler_params=_params("parallel", "arbitrary"),
        name="norm_matmul",
    )(x2, g.reshape(1, d), w_bf)


def _ffn_kernel(x_ref, g_ref, st_ref, wup_ref, wdw_ref, bdw_ref, wdn_ref, y_ref, nst_ref,
                u_sc, act_sc, *, bb, tl, f, cw):
    l = pl.program_id(1)
    m = bb * tl
    d = x_ref.shape[-1]
    pre = SUBLANES - (FFN_CONV_WIDTH - 1)

    @pl.when(l == 0)
    def _():
        u_sc[:, pre:SUBLANES, :] = st_ref[...]

    x = x_ref[...].reshape(m, d)
    hn = _rms(x, g_ref[...]).astype(BF16)
    for c in range(f // cw):
        halves = []
        for col in (c * cw, f + c * cw):
            cols = slice(col, col + cw)
            u_sc[:, SUBLANES:, cols] = _dot(hn, wup_ref[:, cols]).reshape(bb, tl, cw)
            acc = bdw_ref[:, cols]
            for w in range(FFN_CONV_WIDTH):
                acc = acc + wdw_ref[w:w + 1, cols] * u_sc[:, pre + w:pre + w + tl, cols]
            halves.append(acc)
        ya, yb = halves
        act_sc[:, c * cw:(c + 1) * cw] = (ya * jax.nn.sigmoid(ya) * yb).reshape(m, cw).astype(BF16)
    y_ref[...] = (x + _dot(act_sc[...], wdn_ref[...])).reshape(bb, tl, d)

    new = u_sc[:, tl + pre:tl + SUBLANES, :]

    @pl.when(l == pl.num_programs(1) - 1)
    def _():
        nst_ref[...] = new

    u_sc[:, pre:SUBLANES, :] = new


def conv_ffn(x, g, state, wup_bf, wdw, bdw, wdn_bf, *, bb, tl, cw):
    b, l, d = x.shape
    f = wdn_bf.shape[0]
    kern = functools.partial(_ffn_kernel, bb=bb, tl=tl, f=f, cw=cw)
    return pl.pallas_call(
        kern,
        grid=(b // bb, l // tl),
        in_specs=[pl.BlockSpec((bb, tl, d), lambda i, j: (i, j, 0)),
                  _resident((1, d)),
                  pl.BlockSpec((bb, FFN_CONV_WIDTH - 1, 2 * f), lambda i, j: (i, 0, 0)),
                  _resident((d, 2 * f)),
                  _resident((FFN_CONV_WIDTH, 2 * f)),
                  _resident((1, 2 * f)),
                  _resident((f, d))],
        out_specs=[pl.BlockSpec((bb, tl, d), lambda i, j: (i, j, 0)),
                   pl.BlockSpec((bb, FFN_CONV_WIDTH - 1, 2 * f), lambda i, j: (i, 0, 0))],
        out_shape=[jax.ShapeDtypeStruct((b, l, d), F32),
                   jax.ShapeDtypeStruct((b, FFN_CONV_WIDTH - 1, 2 * f), F32)],
        scratch_shapes=[pltpu.VMEM((bb, SUBLANES + tl, 2 * f), F32),
                        pltpu.VMEM((bb * tl, f), BF16)],
        compiler_params=_params("parallel", "arbitrary"),
        name="conv_ffn",
    )(x, g.reshape(1, d), state, wup_bf, wdw, bdw.reshape(1, 2 * f), wdn_bf)


def _conv_mixer_kernel(x_ref, g_ref, st_ref, win_ref, wdw_ref, bdw_ref, lng_ref, lnb_ref, wout_ref,
                       y_ref, nst_ref, buf_sc, conv_sc, act_sc, *, bb, tl, bbk, rb, cbw):
    l = pl.program_id(1)
    m = bb * tl
    d = x_ref.shape[-1]
    c = wout_ref.shape[0]
    hist = CONV_WIDTH - 1
    head = 4 * SUBLANES
    off = head - hist

    @pl.when(l == 0)
    def _():
        buf_sc[:, off:head, :] = st_ref[...]

    x = x_ref[...].reshape(m, d)
    hn = _rms(x, g_ref[...]).astype(BF16)
    ag = _dot(hn, win_ref[...])
    u = ag[:, :c] * jax.nn.sigmoid(ag[:, c:])
    buf_sc[:, head:, :] = u.reshape(bb, tl, c)

    for b0 in range(0, bb, bbk):
        for r0 in range(0, tl, rb):
            lo = head + r0 - SUBLANES
            for c0 in range(0, c, cbw):
                cols = slice(c0, c0 + cbw)
                y = jnp.broadcast_to(bdw_ref[:, cols].reshape(1, 1, cbw), (bbk, rb, cbw))
                for s in range(SUBLANES):
                    z = None
                    for a in range(-(-CONV_WIDTH // SUBLANES)):
                        delay = SUBLANES * a + s
                        if delay >= CONV_WIDTH:
                            continue
                        w = jnp.concatenate([wdw_ref[hist - delay, :, cols]] * (rb // SUBLANES + 1), axis=0)[None]
                        term = w * buf_sc[b0:b0 + bbk, lo - SUBLANES * a:lo - SUBLANES * a + rb + SUBLANES, cols]
                        z = term if z is None else z + term
                    if s:
                        z = pltpu.roll(z, s, 1)
                    y = y + z[:, SUBLANES:, :]
                conv_sc[b0:b0 + bbk, r0:r0 + rb, cols] = y
            acc = conv_sc[b0:b0 + bbk, r0:r0 + rb, :]
            mu = jnp.mean(acc, axis=-1, keepdims=True)
            xc = acc - mu
            var = jnp.mean(xc * xc, axis=-1, keepdims=True)
            yn = xc * lax.rsqrt(var + EPS) * lng_ref[...].reshape(1, 1, c) + lnb_ref[...].reshape(1, 1, c)
            act_sc[b0:b0 + bbk, r0:r0 + rb, :] = (yn * jax.nn.sigmoid(yn)).astype(act_sc.dtype)
    y_ref[...] = (x + _dot(act_sc[...].reshape(m, c).astype(BF16), wout_ref[...])).reshape(bb, tl, d)

    new = buf_sc[:, tl + off:tl + head, :]

    @pl.when(l == pl.num_programs(1) - 1)
    def _():
        nst_ref[...] = new

    buf_sc[:, off:head, :] = new


def conv_mixer(x, g, state, win_bf, wdw, bdw, lng, lnb, wout_bf, *, bb, tl, bbk, rb):
    b, l, d = x.shape
    c = wout_bf.shape[0]
    hist = CONV_WIDTH - 1
    kern = functools.partial(_conv_mixer_kernel, bb=bb, tl=tl, bbk=bbk, rb=rb, cbw=128)
    return pl.pallas_call(
        kern,
        grid=(b // bb, l // tl),
        in_specs=[pl.BlockSpec((bb, tl, d), lambda i, j: (i, j, 0)),
                  _resident((1, d)),
                  pl.BlockSpec((bb, hist, c), lambda i, j: (i, 0, 0)),
                  _resident((d, 2 * c)),
                  _resident((CONV_WIDTH, SUBLANES, c)),
                  _resident((1, c)),
                  _resident((1, c)),
                  _resident((1, c)),
                  _resident((c, d))],
        out_specs=[pl.BlockSpec((bb, tl, d), lambda i, j: (i, j, 0)),
                   pl.BlockSpec((bb, hist, c), lambda i, j: (i, 0, 0))],
        out_shape=[jax.ShapeDtypeStruct((b, l, d), F32),
                   jax.ShapeDtypeStruct((b, hist, c), F32)],
        scratch_shapes=[pltpu.VMEM((bb, 4 * SUBLANES + tl, c), F32),
                        pltpu.VMEM((bb, tl, c), F32),
                        pltpu.VMEM((bb, tl, c), _mm_dtype(rb))],
        compiler_params=_params("parallel", "arbitrary"),
        name="conv_mixer",
    )(x, g.reshape(1, d), state, win_bf, jnp.broadcast_to(wdw[:, None, :], (CONV_WIDTH, SUBLANES, c)),
      bdw.reshape(1, c), lng.reshape(1, c), lnb.reshape(1, c), wout_bf)


def _gla_gate_kernel(g1_ref, w2_ref, b_ref, o_ref):
    z = _dot(g1_ref[...].astype(BF16), w2_ref[...]) + b_ref[...]
    o_ref[...] = _log_sigmoid(z) * (1.0 / GLA_GATE_NORM)


def gla_gate(h2, w2_pad_bf, b):
    m = h2.shape[0]
    dk = w2_pad_bf.shape[1]
    tm = min(m, 1024)
    g1_block = (2 * dk + 2 * GLA_HEADS * GLA_HEAD_V) // LANES
    return pl.pallas_call(
        _gla_gate_kernel,
        grid=(m // tm,),
        in_specs=[pl.BlockSpec((tm, LANES), lambda i: (i, g1_block)),
                  _resident((LANES, dk)),
                  _resident((1, dk))],
        out_specs=pl.BlockSpec((tm, dk), lambda i: (i, 0)),
        out_shape=jax.ShapeDtypeStruct((m, dk), F32),
        compiler_params=_params("parallel"),
        name="gla_gate",
    )(h2, w2_pad_bf, b.reshape(1, dk))


def _gla_rec_kernel(q_ref, k_ref, v_ref, g_ref, s0_ref, o_ref, s_ref, st_sc, *, tl, chunk):
    l = pl.program_id(2)
    mm = _mm_dtype(chunk)

    @pl.when(l == 0)
    def _():
        st_sc[...] = s0_ref[0, 0].T

    q = q_ref[0] * (GLA_HEAD_K ** -0.5)
    k = k_ref[0]
    v = v_ref[0]
    g = g_ref[0]

    row = lax.broadcasted_iota(jnp.int32, (tl, tl), 0)
    col = lax.broadcasted_iota(jnp.int32, (tl, tl), 1)
    tril = ((row // chunk == col // chunk) & (col <= row)).astype(mm)
    gcum = _dot_exact_lhs01(tril, g, mm)

    crow = lax.broadcasted_iota(jnp.int32, (chunk, chunk), 0)
    ccol = lax.broadcasted_iota(jnp.int32, (chunk, chunk), 1)
    causal = ccol <= crow

    st = st_sc[...]
    outs = []
    for c in range(tl // chunk):
        sl = slice(c * chunk, (c + 1) * chunk)
        gc = gcum[sl]
        gl = gc[chunk - 1:chunk]
        qe = (q[sl] * jnp.exp(gc)).astype(mm)
        ke = (k[sl] * jnp.exp(-gc)).astype(mm)
        kd = (k[sl] * jnp.exp(gl - gc)).astype(mm)
        vc = v[sl].astype(mm)
        scores = jnp.where(causal, _dot_nt(qe, ke), 0.0)
        outs.append(_dot_nt(qe, st.astype(mm)) + _dot(scores.astype(mm), vc))
        st = st * jnp.exp(gl) + _dot_tn(vc, kd)
    o_ref[0] = outs[0] if len(outs) == 1 else jnp.concatenate(outs, axis=0)
    st_sc[...] = st

    @pl.when(l == pl.num_programs(2) - 1)
    def _():
        s_ref[0, 0] = st.T


def gla_recurrence(h3, gk3, s0, *, tl, chunk):
    b, l, _ = h3.shape
    hk, hv, nh = GLA_HEAD_K, GLA_HEAD_V, GLA_HEADS
    k_off = (nh * hk) // hk
    v_off = (2 * nh * hk) // hv
    kern = functools.partial(_gla_rec_kernel, tl=tl, chunk=chunk)
    return pl.pallas_call(
        kern,
        grid=(b, nh, l // tl),
        in_specs=[pl.BlockSpec((1, tl, hk), lambda i, h, j: (i, j, h)),
                  pl.BlockSpec((1, tl, hk), lambda i, h, j: (i, j, k_off + h)),
                  pl.BlockSpec((1, tl, hv), lambda i, h, j: (i, j, v_off + h)),
                  pl.BlockSpec((1, tl, hk), lambda i, h, j: (i, j, h)),
                  pl.BlockSpec((1, 1, hk, hv), lambda i, h, j: (i, h, 0, 0))],
        out_specs=[pl.BlockSpec((1, tl, hv), lambda i, h, j: (i, j, h)),
                   pl.BlockSpec((1, 1, hk, hv), lambda i, h, j: (i, h, 0, 0))],
        out_shape=[jax.ShapeDtypeStruct((b, l, nh * hv), F32),
                   jax.ShapeDtypeStruct((b, nh, hk, hv), F32)],
        scratch_shapes=[pltpu.VMEM((hv, hk), F32)],
        compiler_params=_params("parallel", "parallel", "arbitrary"),
        name="gla_recurrence",
    )(h3, h3, h3, gk3, s0)


def _gla_out_kernel(x_ref, o_ref, r_ref, gn_ref, w_ref, y_ref):
    o = o_ref[...]
    gn = gn_ref[...]
    parts = []
    for h in range(GLA_HEADS):
        parts.append(_rms(o[:, h * GLA_HEAD_V:(h + 1) * GLA_HEAD_V], gn))
    on = jnp.concatenate(parts, axis=-1)
    r = r_ref[...]
    gated = (on * (r * jax.nn.sigmoid(r))).astype(BF16)
    y_ref[...] = x_ref[...] + _dot(gated, w_ref[...])


def gla_out(x2, o2, h2, gn, wout_bf):
    m, d = x2.shape
    dv = o2.shape[1]
    tm = min(m, 512)
    r_block = (2 * GLA_HEADS * GLA_HEAD_K + dv) // dv
    return pl.pallas_call(
        _gla_out_kernel,
        grid=(m // tm,),
        in_specs=[pl.BlockSpec((tm, d), lambda i: (i, 0)),
                  pl.BlockSpec((tm, dv), lambda i: (i, 0)),
                  pl.BlockSpec((tm, dv), lambda i: (i, r_block)),
                  _resident((1, GLA_HEAD_V)),
                  _resident((dv, d))],
        out_specs=pl.BlockSpec((tm, d), lambda i: (i, 0)),
        out_shape=jax.ShapeDtypeStruct((m, d), F32),
        compiler_params=_params("parallel"),
        name="gla_out",
    )(x2, o2, h2, gn.reshape(1, GLA_HEAD_V), wout_bf)


def _head_norm(x, g2):
    lo = lax.broadcasted_iota(jnp.int32, (1, LANES), 1) < FOX_HEAD_DIM
    parts = []
    for j in range(x.shape[-1] // LANES):
        xb = x[:, j * LANES:(j + 1) * LANES]
        sq = xb * xb
        s_lo = jnp.sum(jnp.where(lo, sq, 0.0), axis=-1, keepdims=True)
        s_hi = jnp.sum(jnp.where(lo, 0.0, sq), axis=-1, keepdims=True)
        ms = jnp.where(lo, s_lo, s_hi) * (1.0 / FOX_HEAD_DIM)
        parts.append(xb * lax.rsqrt(ms + EPS) * g2)
    return jnp.concatenate(parts, axis=-1)


def _lane_col(x, h):
    return jnp.broadcast_to(x[:, h:h + 1], x.shape)


def _fox_prep_prompt_kernel(q_ref, k_ref, v_ref, fl_ref, qg_ref, kg_ref, bf_ref,
                            qa_ref, kn_ref, ka_ref, vo_ref, va_ref, lf_ref, carry_sc, *, tl):
    l = pl.program_id(1)

    @pl.when(l == 0)
    def _():
        carry_sc[...] = jnp.zeros_like(carry_sc)

    qn = _head_norm(q_ref[0], qg_ref[...]) * (FOX_HEAD_DIM ** -0.5 * LOG2E)
    kn = _head_norm(k_ref[0], kg_ref[...])
    v = v_ref[0]
    kn_ref[0] = kn
    vo_ref[0] = v

    lf = _log_sigmoid(fl_ref[0] + bf_ref[...])
    lf_ref[0] = lf[:, :FOX_HEADS]
    row = lax.broadcasted_iota(jnp.int32, (tl, tl), 0)
    col = lax.broadcasted_iota(jnp.int32, (tl, tl), 1)
    c = _dot_exact_lhs01((col <= row).astype(BF16), lf, BF16) + carry_sc[...]
    carry_sc[...] = c[tl - 1:tl]
    cpos = _split3(c * LOG2E)

    lane = lax.broadcasted_iota(jnp.int32, (1, LANES), 1)
    for h in range(FOX_HEADS):
        src = slice((h // 2) * LANES, (h // 2 + 1) * LANES)
        dst = slice(h * LANES, (h + 1) * LANES)
        own = (lane < FOX_HEAD_DIM) if h % 2 == 0 else (lane >= FOX_HEAD_DIM)
        a0 = FOX_HEAD_DIM if h % 2 == 0 else 0
        qaug = jnp.where((lane >= a0 + 3) & (lane < a0 + 6), 1.0, 0.0)
        kaug = jnp.where((lane >= a0) & (lane < a0 + 3), 1.0, 0.0)
        for i in range(3):
            piece = _lane_col(cpos[i], h)
            qaug = jnp.where(lane == a0 + i, piece, qaug)
            kaug = jnp.where(lane == a0 + 3 + i, -piece, kaug)
        vaug = jnp.where(lane == a0, 1.0, 0.0)
        qa_ref[0, :, dst] = jnp.where(own, qn[:, src], qaug).astype(BF16)
        ka_ref[0, :, dst] = jnp.where(own, kn[:, src], kaug).astype(BF16)
        va_ref[0, :, dst] = jnp.where(own, v[:, src], vaug).astype(BF16)


def fox_prep_prompt(h3, qg, kg, bf, *, tl):
    b, l, _ = h3.shape
    d = FOX_HEADS * FOX_HEAD_DIM
    da = FOX_HEADS * LANES
    fl_block = (4 * d) // LANES
    qg2 = jnp.tile(qg, 2).reshape(1, LANES)
    kg2 = jnp.tile(kg, 2).reshape(1, LANES)
    bf_pad = jnp.pad(bf, (0, LANES - FOX_HEADS)).reshape(1, LANES)
    row3 = lambda i, j: (i, j, 0)
    kern = functools.partial(_fox_prep_prompt_kernel, tl=tl)
    return pl.pallas_call(
        kern,
        grid=(b, l // tl),
        in_specs=[pl.BlockSpec((1, tl, d), lambda i, j: (i, j, 0)),
                  pl.BlockSpec((1, tl, d), lambda i, j: (i, j, 1)),
                  pl.BlockSpec((1, tl, d), lambda i, j: (i, j, 2)),
                  pl.BlockSpec((1, tl, LANES), lambda i, j: (i, j, fl_block)),
                  _resident((1, LANES)), _resident((1, LANES)), _resident((1, LANES))],
        out_specs=[pl.BlockSpec((1, tl, da), row3),
                   pl.BlockSpec((1, tl, d), row3),
                   pl.BlockSpec((1, tl, da), row3),
                   pl.BlockSpec((1, tl, d), row3),
                   pl.BlockSpec((1, tl, da), row3),
                   pl.BlockSpec((1, tl, FOX_HEADS), row3)],
        out_shape=[jax.ShapeDtypeStruct((b, l, da), BF16),
                   jax.ShapeDtypeStruct((b, l, d), F32),
                   jax.ShapeDtypeStruct((b, l, da), BF16),
                   jax.ShapeDtypeStruct((b, l, d), F32),
                   jax.ShapeDtypeStruct((b, l, da), BF16),
                   jax.ShapeDtypeStruct((b, l, FOX_HEADS), F32)],
        scratch_shapes=[pltpu.VMEM((1, LANES), F32)],
        compiler_params=_params("parallel", "arbitrary"),
        name="fox_prep_prompt",
    )(h3, h3, h3, h3, qg2, kg2, bf_pad)


def _fox_prep_sample_kernel(q_ref, k_ref, v_ref, fl_ref, qg_ref, kg_ref, bf_ref, qo_ref, kn_ref, vo_ref, lf_ref):
    qo_ref[...] = _head_norm(q_ref[...], qg_ref[...]) * (FOX_HEAD_DIM ** -0.5)
    kn_ref[...] = _head_norm(k_ref[...], kg_ref[...])
    vo_ref[...] = v_ref[...]
    lf_ref[...] = _log_sigmoid(fl_ref[...] + bf_ref[...])[:, :FOX_HEADS]


def fox_prep_sample(h2, qg, kg, bf):
    m = h2.shape[0]
    d = FOX_HEADS * FOX_HEAD_DIM
    fl_block = (4 * d) // LANES
    qg2 = jnp.tile(qg, 2).reshape(1, LANES)
    kg2 = jnp.tile(kg, 2).reshape(1, LANES)
    bf_pad = jnp.pad(bf, (0, LANES - FOX_HEADS)).reshape(1, LANES)
    full = lambda i: (0, 0)
    return pl.pallas_call(
        _fox_prep_sample_kernel,
        grid=(1,),
        in_specs=[pl.BlockSpec((m, d), lambda i: (0, 0)),
                  pl.BlockSpec((m, d), lambda i: (0, 1)),
                  pl.BlockSpec((m, d), lambda i: (0, 2)),
                  pl.BlockSpec((m, LANES), lambda i: (0, fl_block)),
                  _resident((1, LANES)), _resident((1, LANES)), _resident((1, LANES))],
        out_specs=[pl.BlockSpec((m, d), full), pl.BlockSpec((m, d), full), pl.BlockSpec((m, d), full),
                   pl.BlockSpec((m, FOX_HEADS), full)],
        out_shape=[jax.ShapeDtypeStruct((m, d), F32), jax.ShapeDtypeStruct((m, d), F32),
                   jax.ShapeDtypeStruct((m, d), F32), jax.ShapeDtypeStruct((m, FOX_HEADS), F32)],
        compiler_params=_params("arbitrary"),
        name="fox_prep_sample",
    )(h2, h2, h2, h2, qg2, kg2, bf_pad)


def _flash_kernel(qt_ref, kt_ref, q_ref, k_ref, v_ref, o_ref, m_sc, acc_sc, *, t):
    step = pl.program_id(1)
    qi = qt_ref[step]
    ki = kt_ref[step]
    lo = lax.broadcasted_iota(jnp.int32, (1, LANES), 1) < FOX_HEAD_DIM

    @pl.when(ki == 0)
    def _():
        m_sc[...] = jnp.full_like(m_sc, -jnp.inf)
        acc_sc[...] = jnp.zeros_like(acc_sc)

    def head_update(h, diagonal):
        cols = pl.ds(pl.multiple_of(h * LANES, LANES), LANES)
        s = _dot_nt(q_ref[0, :, cols], k_ref[0, :, cols])
        if diagonal:
            row = lax.broadcasted_iota(jnp.int32, (t, t), 0)
            col = lax.broadcasted_iota(jnp.int32, (t, t), 1)
            s = jnp.where(col <= row, s, -jnp.inf)
        m_prev = m_sc[h]
        m_new = jnp.maximum(m_prev, jnp.max(s, axis=-1, keepdims=True))
        alpha = jnp.exp2(m_prev - m_new)
        p = jnp.exp2(s - jnp.concatenate([m_new] * (t // LANES), axis=1))
        acc_sc[h] = acc_sc[h] * alpha + _dot(p.astype(BF16), v_ref[0, :, cols])
        m_sc[h] = m_new

    @pl.when(ki < qi)
    def _():
        def body(j, carry):
            head_update(2 * j, False)
            head_update(2 * j + 1, False)
            return carry

        lax.fori_loop(0, FOX_HEADS // 2, body, 0, unroll=2)

    @pl.when(ki == qi)
    def _():
        def body(j, carry):
            head_update(2 * j, True)
            head_update(2 * j + 1, True)
            a0 = acc_sc[2 * j]
            a1 = acc_sc[2 * j + 1]
            out = jnp.where(lo, a0 / a0[:, FOX_HEAD_DIM:FOX_HEAD_DIM + 1], a1 / a1[:, 0:1])
            o_ref[0, :, pl.ds(pl.multiple_of(j * LANES, LANES), LANES)] = out
            return carry

        lax.fori_loop(0, FOX_HEADS // 2, body, 0)


def fox_flash(qa, ka, va, *, t):
    b, l, da = qa.shape
    d = FOX_HEADS * FOX_HEAD_DIM
    n = l // t
    qi = np.concatenate([np.full(i + 1, i, np.int32) for i in range(n)])
    ki = np.concatenate([np.arange(i + 1, dtype=np.int32) for i in range(n)])
    kern = functools.partial(_flash_kernel, t=t)
    grid_spec = pltpu.PrefetchScalarGridSpec(
        num_scalar_prefetch=2,
        grid=(b, len(qi)),
        in_specs=[pl.BlockSpec((1, t, da), lambda i, s, qt, kt: (i, qt[s], 0)),
                  pl.BlockSpec((1, t, da), lambda i, s, qt, kt: (i, kt[s], 0)),
                  pl.BlockSpec((1, t, da), lambda i, s, qt, kt: (i, kt[s], 0))],
        out_specs=pl.BlockSpec((1, t, d), lambda i, s, qt, kt: (i, qt[s], 0)),
        scratch_shapes=[pltpu.VMEM((FOX_HEADS, t, LANES), F32),
                        pltpu.VMEM((FOX_HEADS, t, LANES), F32)])
    return pl.pallas_call(
        kern,
        grid_spec=grid_spec,
        out_shape=jax.ShapeDtypeStruct((b, l, d), F32),
        compiler_params=_params("parallel", "arbitrary"),
        name="fox_flash",
    )(jnp.asarray(qi), jnp.asarray(ki), qa, ka, va)


def _decode_kernel(pt_ref, q_ref, kn_ref, vn_ref, lfn_ref, *refs, lq, page, pg):
    del pt_ref
    kt_refs, vt_refs, lft_refs = refs[:pg], refs[pg:2 * pg], refs[2 * pg:3 * pg]
    o_ref = refs[3 * pg]
    qbd_sc, m_sc, l_sc, acc_sc, cq_sc, carry_sc = refs[3 * pg + 1:]
    p = pl.program_id(1)
    nh, hd = FOX_HEADS, FOX_HEAD_DIM
    rows = nh * lq
    d = nh * hd

    uj = lax.broadcasted_iota(jnp.int32, (page, 2 * page), 0)
    us = lax.broadcasted_iota(jnp.int32, (page, 2 * page), 1)
    u2 = ((uj > us) | (us >= page)).astype(BF16)

    def later_sums(x, keep):
        out = jnp.zeros((x.shape[0], 2 * page), F32)
        for piece in _split3(x):
            if keep is not None:
                piece = jnp.where(keep, piece, 0.0)
            out = out + _dot(piece.astype(BF16), u2)
        return out

    def head_rows(x):
        return jnp.concatenate([jnp.broadcast_to(x[h:h + 1], (lq, x.shape[1])) for h in range(nh)], axis=0)

    @pl.when(p == 0)
    def _():
        qt = jnp.concatenate([q_ref[0]] * nh, axis=0)
        rr = lax.broadcasted_iota(jnp.int32, (rows, d), 0)
        cc = lax.broadcasted_iota(jnp.int32, (rows, d), 1)
        qbd_sc[...] = jnp.where(rr // lq == cc // hd, qt, 0.0)

        pad = page - lq
        kn = jnp.concatenate([kn_ref[0], jnp.zeros((pad, d), F32)], axis=0)
        vn = jnp.concatenate([vn_ref[0], jnp.zeros((pad, d), F32)], axis=0)
        lfn = jnp.concatenate([lfn_ref[0], jnp.zeros((pad, nh), F32)], axis=0)
        er = lax.broadcasted_iota(jnp.int32, (rows, nh), 0)
        ec = lax.broadcasted_iota(jnp.int32, (rows, nh), 1)
        expand = (er // lq == ec).astype(BF16)
        t = lax.broadcasted_iota(jnp.int32, (rows, page), 0) % lq
        j = lax.broadcasted_iota(jnp.int32, (rows, page), 1)
        sums = jnp.zeros((rows, 2 * page), F32)
        for piece in _split3(lfn):
            e = jnp.where(j <= t, _dot_nt(expand, piece.astype(BF16)), 0.0)
            sums = sums + _dot(e.astype(BF16), u2)
        cq_sc[...] = sums[:, page:]
        s = jnp.where(j <= t, _dot_nt(qbd_sc[...], kn) + sums[:, :page], -jnp.inf)
        m_cur = jnp.max(s, axis=-1, keepdims=True)
        pexp = jnp.exp(s - m_cur)
        m_sc[...] = jnp.broadcast_to(m_cur, (rows, LANES))
        l_sc[...] = jnp.broadcast_to(jnp.sum(pexp, axis=-1, keepdims=True), (rows, LANES))
        acc_sc[...] = _dot(pexp, vn)
        carry_sc[...] = jnp.zeros_like(carry_sc)

    lft = jnp.concatenate([r[0] for r in lft_refs], axis=0)
    sums = later_sums(lft, None)
    carry = carry_sc[...]
    tiles = []
    for i in range(pg):
        blk = slice(i * nh, (i + 1) * nh)
        bias = head_rows(sums[blk, :page] + carry) + cq_sc[...]
        carry = carry + sums[blk, page:]
        tiles.append(_dot(qbd_sc[...], kt_refs[i][0]) + bias)
    carry_sc[...] = carry
    s = jnp.concatenate(tiles, axis=1)
    m_prev = m_sc[...]
    m_new = jnp.maximum(m_prev, jnp.max(s, axis=-1, keepdims=True))
    alpha = jnp.exp(m_prev - m_new)
    pexp = jnp.exp(s - jnp.concatenate([m_new] * pg, axis=1))
    l_sc[...] = alpha * l_sc[...] + jnp.sum(pexp, axis=-1, keepdims=True)
    m_sc[...] = m_new
    pv = _dot_nt(pexp[:, :page], vt_refs[0][0])
    for i in range(1, pg):
        pv = pv + _dot_nt(pexp[:, i * page:(i + 1) * page], vt_refs[i][0])
    for c in range(d // LANES):
        cols = slice(c * LANES, (c + 1) * LANES)
        acc_sc[:, cols] = acc_sc[:, cols] * alpha + pv[:, cols]

    @pl.when(p == pl.num_programs(1) - 1)
    def _():
        rr = lax.broadcasted_iota(jnp.int32, (rows, d), 0)
        cc = lax.broadcasted_iota(jnp.int32, (rows, d), 1)
        inv = 1.0 / l_sc[...]
        parts = [acc_sc[:, c * LANES:(c + 1) * LANES] * inv for c in range(d // LANES)]
        on = jnp.where(rr // lq == cc // hd, jnp.concatenate(parts, axis=-1), 0.0)
        o_ref[0] = jnp.sum(on.reshape(nh, lq, d), axis=0)


def fox_decode(q3, kn3, vn3, lfn3, kpool_t, vpool_t, lfpool_t, page_table, *, pg):
    b, lq, d = q3.shape
    nh = FOX_HEADS
    n_pages = page_table.shape[1]
    page = lfpool_t.shape[2]
    rows = nh * lq
    assert page == LANES and rows == LANES and n_pages % pg == 0

    def new_map(i, p, pt):
        return (i, 0, 0)

    def page_map(k):
        return lambda i, p, pt: (pt[i, n_pages - 1 - (p * pg + k)], 0, 0)

    kern = functools.partial(_decode_kernel, lq=lq, page=page, pg=pg)
    grid_spec = pltpu.PrefetchScalarGridSpec(
        num_scalar_prefetch=1,
        grid=(b, n_pages // pg),
        in_specs=[pl.BlockSpec((1, lq, d), new_map),
                  pl.BlockSpec((1, lq, d), new_map),
                  pl.BlockSpec((1, lq, d), new_map),
                  pl.BlockSpec((1, lq, nh), new_map)]
                 + [pl.BlockSpec((1, d, page), page_map(k)) for k in range(pg)]
                 + [pl.BlockSpec((1, d, page), page_map(k)) for k in range(pg)]
                 + [pl.BlockSpec((1, nh, page), page_map(k)) for k in range(pg)],
        out_specs=pl.BlockSpec((1, lq, d), new_map),
        scratch_shapes=[pltpu.VMEM((rows, d), F32),
                        pltpu.VMEM((rows, LANES), F32),
                        pltpu.VMEM((rows, LANES), F32),
                        pltpu.VMEM((rows, d), F32),
                        pltpu.VMEM((rows, LANES), F32),
                        pltpu.VMEM((nh, LANES), F32)])
    return pl.pallas_call(
        kern,
        grid_spec=grid_spec,
        out_shape=jax.ShapeDtypeStruct((b, lq, d), F32),
        compiler_params=_params("parallel", "arbitrary"),
        name="fox_decode",
    )(page_table, q3, kn3, vn3, lfn3, *([kpool_t] * pg), *([vpool_t] * pg), *([lfpool_t] * pg))


def _fox_out_kernel(x_ref, o_ref, og_ref, w_ref, y_ref):
    gated = (o_ref[...] * jax.nn.sigmoid(og_ref[...])).astype(BF16)
    y_ref[...] = x_ref[...] + _dot(gated, w_ref[...])


def fox_out(x2, o2, h2, wout_bf):
    m, d = x2.shape
    tm = min(m, 512)
    return pl.pallas_call(
        _fox_out_kernel,
        grid=(m // tm,),
        in_specs=[pl.BlockSpec((tm, d), lambda i: (i, 0)),
                  pl.BlockSpec((tm, d), lambda i: (i, 0)),
                  pl.BlockSpec((tm, d), lambda i: (i, 3)),
                  _resident((d, d))],
        out_specs=pl.BlockSpec((tm, d), lambda i: (i, 0)),
        out_shape=jax.ShapeDtypeStruct((m, d), F32),
        compiler_params=_params("parallel"),
        name="fox_out",
    )(x2, o2, h2, wout_bf)


def _tiles(x):
    b, l, _ = x.shape
    if l >= 512:
        return 1, 512
    return b, l


def _conv_layer(x, g, state, prm):
    bb, tl = _tiles(x)
    tl = min(tl, 256)
    bbk, rb = (1, 64) if tl >= 64 else (64 // tl, tl)
    return conv_mixer(x, g, state, *prm, bb=bb, tl=tl, bbk=bbk, rb=rb)


def _ffn_layer(x, g, state, prm):
    bb, tl = _tiles(x)
    return conv_ffn(x, g, state, *prm, bb=bb, tl=tl, cw=FFN_CHUNK)


def _gla_layer(x, g, s0, prm):
    win_bf, w2_bf, b_gk, gn, wout_bf = prm
    b, l, d = x.shape
    x2 = x.reshape(b * l, d)
    h2 = norm_matmul(x2, g, win_bf, 640)
    gk2 = gla_gate(h2, w2_bf, b_gk)
    chunk = GLA_CHUNK if l % GLA_CHUNK == 0 else l
    tl = 4 * chunk if l % (4 * chunk) == 0 else chunk
    o3, s = gla_recurrence(h2.reshape(b, l, -1), gk2.reshape(b, l, -1), s0, tl=tl, chunk=chunk)
    y2 = gla_out(x2, o3.reshape(b * l, -1), h2, gn, wout_bf)
    return y2.reshape(b, l, d), s


def _fox_layer(x, g, cache, prm):
    win_bf, b_f, qg, kg, wout_bf = prm
    b, l, d = x.shape
    x2 = x.reshape(b * l, d)
    h2 = norm_matmul(x2, g, win_bf, 1408)
    if cache is None:
        qa, kn, ka, vo, va, lf = fox_prep_prompt(h2.reshape(b, l, -1), qg, kg, b_f, tl=512)
        o3 = fox_flash(qa, ka, va, t=512)
    else:
        kpool_t, vpool_t, lfpool_t, page_table = cache
        q2, kn, vo, lf = fox_prep_sample(h2, qg, kg, b_f)
        o3 = fox_decode(q2.reshape(b, l, d), kn.reshape(b, l, d), vo.reshape(b, l, d), lf.reshape(b, l, FOX_HEADS),
                        kpool_t, vpool_t, lfpool_t, page_table, pg=4)
    y2 = fox_out(x2, o3.reshape(b * l, d), h2, wout_bf)
    shp = (b, l, FOX_HEADS, FOX_HEAD_DIM)
    return y2.reshape(b, l, d), kn.reshape(shp), vo.reshape(shp), lf.reshape(b, l, FOX_HEADS)


def _pad_cols(w, n):
    return jnp.pad(w, ((0, 0), (0, n - w.shape[1])))


def kernel(x_prompt, x_sample, state_conv, state_gla, cache_fox_k, cache_fox_v, cache_fox_logf, state_ffn_conv, page_table,
           norm_mix_g, norm_ffn_g, w_conv_in, w_conv_dw, b_conv_dw, conv_ln_g, conv_ln_b, w_conv_out,
           w_gla_in, w_gla_gk2, b_gla_gk, gla_norm_g, w_gla_out,
           w_fox_in, b_fox_f, fox_qn_g, fox_kn_g, w_fox_out,
           w_ffn_up, w_ffn_dw, b_ffn_dw, w_ffn_down):
    depth = norm_mix_g.shape[0]
    bp = x_prompt.shape[0]
    d = x_prompt.shape[-1]
    xp, xs = x_prompt, x_sample
    conv_p, conv_s, gla_p, gla_s = [], [], [], []
    fk_p, fk_s, fv_p, fv_s, fl_p, fl_s = [], [], [], [], [], []
    ffn_p, ffn_s = [], []
    for i in range(depth):
        m, j = i % 3, i // 3
        if m == 0:
            prm = (w_conv_in[j].astype(BF16), w_conv_dw[j], b_conv_dw[j], conv_ln_g[j], conv_ln_b[j],
                   w_conv_out[j].astype(BF16))
            zero = jnp.zeros((bp,) + state_conv.shape[2:], F32)
            xp, st_p = _conv_layer(xp, norm_mix_g[i], zero, prm)
            xs, st_s = _conv_layer(xs, norm_mix_g[i], state_conv[j], prm)
            conv_p.append(st_p)
            conv_s.append(st_s)
        elif m == 1:
            n_in = w_gla_in.shape[-1]
            n_pad = -(-n_in // (5 * LANES)) * (5 * LANES)
            w2 = jnp.pad(w_gla_gk2[j], ((0, LANES - GLA_GATE_RANK), (0, 0))).astype(BF16)
            prm = (_pad_cols(w_gla_in[j], n_pad).astype(BF16), w2, b_gla_gk[j], gla_norm_g[j],
                   w_gla_out[j].astype(BF16))
            zero = jnp.zeros((bp,) + state_gla.shape[2:], F32)
            xp, st_p = _gla_layer(xp, norm_mix_g[i], zero, prm)
            xs, st_s = _gla_layer(xs, norm_mix_g[i], state_gla[j], prm)
            gla_p.append(st_p)
            gla_s.append(st_s)
        else:
            n_pad = 4 * d + LANES
            prm = (_pad_cols(w_fox_in[j], n_pad).astype(BF16), b_fox_f[j], fox_qn_g[j], fox_kn_g[j],
                   w_fox_out[j].astype(BF16))
            n_pool, page = cache_fox_k.shape[1], cache_fox_k.shape[2]
            cache = (jnp.transpose(cache_fox_k[j], (0, 2, 3, 1)).reshape(n_pool, d, page),
                     jnp.transpose(cache_fox_v[j], (0, 2, 3, 1)).reshape(n_pool, d, page),
                     jnp.transpose(cache_fox_logf[j], (0, 2, 1)), page_table)
            xp, kp, vp, lp = _fox_layer(xp, norm_mix_g[i], None, prm)
            xs, kn, vn, ln = _fox_layer(xs, norm_mix_g[i], cache, prm)
            fk_p.append(kp)
            fk_s.append(kn)
            fv_p.append(vp)
            fv_s.append(vn)
            fl_p.append(lp)
            fl_s.append(ln)
        fprm = (w_ffn_up[i].astype(BF16), w_ffn_dw[i], b_ffn_dw[i], w_ffn_down[i].astype(BF16))
        zero = jnp.zeros((bp,) + state_ffn_conv.shape[2:], F32)
        xp, st_p = _ffn_layer(xp, norm_ffn_g[i], zero, fprm)
        xs, st_s = _ffn_layer(xs, norm_ffn_g[i], state_ffn_conv[i], fprm)
        ffn_p.append(st_p)
        ffn_s.append(st_s)
    return (xp, xs, jnp.stack(conv_p), jnp.stack(conv_s), jnp.stack(gla_p), jnp.stack(gla_s),
            jnp.stack(fk_p), jnp.stack(fk_s), jnp.stack(fv_p), jnp.stack(fv_s), jnp.stack(fl_p), jnp.stack(fl_s),
            jnp.stack(ffn_p), jnp.stack(ffn_s))
```

```python
import functools

import jax
import jax.numpy as jnp
import numpy as np
from jax import lax
from jax.experimental import pallas as pl
from jax.experimental.pallas import tpu as pltpu

F32 = jnp.float32
BF16 = jnp.bfloat16
EPS = 1e-6
LOG2E = 1.4426950408889634

LANES = 128
SUBLANES = 8
VMEM_LIMIT_BYTES = 58 * 1024 * 1024

GLA_HEADS = 4
GLA_HEAD_K = 128
GLA_HEAD_V = 256
GLA_GATE_RANK = 16
GLA_GATE_NORM = 16.0
GLA_CHUNK = 64
FOX_HEADS = 16
FOX_HEAD_DIM = 64
CONV_WIDTH = 31
FFN_CONV_WIDTH = 3
FFN_CHUNK = 256


def _params(*sem):
    return pltpu.CompilerParams(dimension_semantics=sem, vmem_limit_bytes=VMEM_LIMIT_BYTES)


def _resident(shape):
    nd = len(shape)
    return pl.BlockSpec(shape, lambda *_: (0,) * nd, pipeline_mode=pl.Buffered(1))


def _dot(a, b):
    return jnp.dot(a, b, preferred_element_type=F32)


def _dot_nt(a, b):
    return lax.dot_general(a, b, (((1,), (1,)), ((), ())), preferred_element_type=F32)


def _dot_tn(a, b):
    return lax.dot_general(a, b, (((0,), (0,)), ((), ())), preferred_element_type=F32)


def _split3(x):
    hi = x.astype(BF16).astype(F32)
    r = x - hi
    mid = r.astype(BF16).astype(F32)
    lo = (r - mid).astype(BF16).astype(F32)
    return hi, mid, lo


def _dot_exact_lhs01(t, x, mm):
    hi, mid, lo = _split3(x)
    return _dot(t, hi.astype(mm)) + _dot(t, mid.astype(mm)) + _dot(t, lo.astype(mm))


def _rms(x, g):
    return x * lax.rsqrt(jnp.mean(x * x, axis=-1, keepdims=True) + EPS) * g


def _log_sigmoid(z):
    return jnp.minimum(z, 0.0) - jnp.log1p(jnp.exp(-jnp.abs(z)))


def _mm_dtype(rows):
    return BF16 if rows % 16 == 0 else F32


def _norm_matmul_kernel(x_ref, g_ref, w_ref, o_ref, xn_sc):
    @pl.when(pl.program_id(1) == 0)
    def _():
        xn_sc[...] = _rms(x_ref[...], g_ref[...]).astype(BF16)

    o_ref[...] = _dot(xn_sc[...], w_ref[...])


def norm_matmul(x2, g, w_bf, tn):
    m, d = x2.shape
    n = w_bf.shape[1]
    tm = min(m, 1024)
    return pl.pallas_call(
        _norm_matmul_kernel,
        grid=(m // tm, n // tn),
        in_specs=[pl.BlockSpec((tm, d), lambda i, j: (i, 0)),
                  pl.BlockSpec((1, d), lambda i, j: (0, 0)),
                  pl.BlockSpec((d, tn), lambda i, j: (0, j))],
        out_specs=pl.BlockSpec((tm, tn), lambda i, j: (i, j)),
        out_shape=jax.ShapeDtypeStruct((m, n), F32),
        scratch_shapes=[pltpu.VMEM((tm, d), BF16)],
        compiler_params=_params("parallel", "arbitrary"),
        name="norm_matmul",
    )(x2, g.reshape(1, d), w_bf)


def _ffn_kernel(x_ref, g_ref, st_ref, wup_ref, wdw_ref, bdw_ref, wdn_ref, y_ref, nst_ref,
                u_sc, act_sc, *, bb, tl, f, cw):
    l = pl.program_id(1)
    m = bb * tl
    d = x_ref.shape[-1]
    pre = SUBLANES - (FFN_CONV_WIDTH - 1)

    @pl.when(l == 0)
    def _():
        u_sc[:, pre:SUBLANES, :] = st_ref[...]

    x = x_ref[...].reshape(m, d)
    hn = _rms(x, g_ref[...]).astype(BF16)
    for c in range(f // cw):
        halves = []
        for col in (c * cw, f + c * cw):
            cols = slice(col, col + cw)
            u_sc[:, SUBLANES:, cols] = _dot(hn, wup_ref[:, cols]).reshape(bb, tl, cw)
            acc = bdw_ref[:, cols]
            for w in range(FFN_CONV_WIDTH):
                acc = acc + wdw_ref[w:w + 1, cols] * u_sc[:, pre + w:pre + w + tl, cols]
            halves.append(acc)
        ya, yb = halves
        act_sc[:, c * cw:(c + 1) * cw] = (ya * jax.nn.sigmoid(ya) * yb).reshape(m, cw).astype(BF16)
    y_ref[...] = (x + _dot(act_sc[...], wdn_ref[...])).reshape(bb, tl, d)

    new = u_sc[:, tl + pre:tl + SUBLANES, :]

    @pl.when(l == pl.num_programs(1) - 1)
    def _():
        nst_ref[...] = new

    u_sc[:, pre:SUBLANES, :] = new


def conv_ffn(x, g, state, wup_bf, wdw, bdw, wdn_bf, *, bb, tl, cw):
    b, l, d = x.shape
    f = wdn_bf.shape[0]
    kern = functools.partial(_ffn_kernel, bb=bb, tl=tl, f=f, cw=cw)
    return pl.pallas_call(
        kern,
        grid=(b // bb, l // tl),
        in_specs=[pl.BlockSpec((bb, tl, d), lambda i, j: (i, j, 0)),
                  _resident((1, d)),
                  pl.BlockSpec((bb, FFN_CONV_WIDTH - 1, 2 * f), lambda i, j: (i, 0, 0)),
                  _resident((d, 2 * f)),
                  _resident((FFN_CONV_WIDTH, 2 * f)),
                  _resident((1, 2 * f)),
                  _resident((f, d))],
        out_specs=[pl.BlockSpec((bb, tl, d), lambda i, j: (i, j, 0)),
                   pl.BlockSpec((bb, FFN_CONV_WIDTH - 1, 2 * f), lambda i, j: (i, 0, 0))],
        out_shape=[jax.ShapeDtypeStruct((b, l, d), F32),
                   jax.ShapeDtypeStruct((b, FFN_CONV_WIDTH - 1, 2 * f), F32)],
        scratch_shapes=[pltpu.VMEM((bb, SUBLANES + tl, 2 * f), F32),
                        pltpu.VMEM((bb * tl, f), BF16)],
        compiler_params=_params("parallel", "arbitrary"),
        name="conv_ffn",
    )(x, g.reshape(1, d), state, wup_bf, wdw, bdw.reshape(1, 2 * f), wdn_bf)


def _conv_mixer_kernel(x_ref, g_ref, st_ref, win_ref, wdw_ref, bdw_ref, lng_ref, lnb_ref, wout_ref,
                       y_ref, nst_ref, buf_sc, conv_sc, act_sc, *, bb, tl, bbk, rb, cbw):
    l = pl.program_id(1)
    m = bb * tl
    d = x_ref.shape[-1]
    c = wout_ref.shape[0]
    hist = CONV_WIDTH - 1
    head = 4 * SUBLANES
    off = head - hist

    @pl.when(l == 0)
    def _():
        buf_sc[:, off:head, :] = st_ref[...]

    x = x_ref[...].reshape(m, d)
    hn = _rms(x, g_ref[...]).astype(BF16)
    ag = _dot(hn, win_ref[...])
    u = ag[:, :c] * jax.nn.sigmoid(ag[:, c:])
    buf_sc[:, head:, :] = u.reshape(bb, tl, c)

    for b0 in range(0, bb, bbk):
        for r0 in range(0, tl, rb):
            lo = head + r0 - SUBLANES
            for c0 in range(0, c, cbw):
                cols = slice(c0, c0 + cbw)
                y = jnp.broadcast_to(bdw_ref[:, cols].reshape(1, 1, cbw), (bbk, rb, cbw))
                for s in range(SUBLANES):
                    z = None
                    for a in range(-(-CONV_WIDTH // SUBLANES)):
                        delay = SUBLANES * a + s
                        if delay >= CONV_WIDTH:
                            continue
                        w = jnp.concatenate([wdw_ref[hist - delay, :, cols]] * (rb // SUBLANES + 1), axis=0)[None]
                        term = w * buf_sc[b0:b0 + bbk, lo - SUBLANES * a:lo - SUBLANES * a + rb + SUBLANES, cols]
                        z = term if z is None else z + term
                    if s:
                        z = pltpu.roll(z, s, 1)
                    y = y + z[:, SUBLANES:, :]
                conv_sc[b0:b0 + bbk, r0:r0 + rb, cols] = y
            acc = conv_sc[b0:b0 + bbk, r0:r0 + rb, :]
            mu = jnp.mean(acc, axis=-1, keepdims=True)
            xc = acc - mu
            var = jnp.mean(xc * xc, axis=-1, keepdims=True)
            yn = xc * lax.rsqrt(var + EPS) * lng_ref[...].reshape(1, 1, c) + lnb_ref[...].reshape(1, 1, c)
            act_sc[b0:b0 + bbk, r0:r0 + rb, :] = (yn * jax.nn.sigmoid(yn)).astype(act_sc.dtype)
    y_ref[...] = (x + _dot(act_sc[...].reshape(m, c).astype(BF16), wout_ref[...])).reshape(bb, tl, d)

    new = buf_sc[:, tl + off:tl + head, :]

    @pl.when(l == pl.num_programs(1) - 1)
    def _():
        nst_ref[...] = new

    buf_sc[:, off:head, :] = new


def conv_mixer(x, g, state, win_bf, wdw, bdw, lng, lnb, wout_bf, *, bb, tl, bbk, rb):
    b, l, d = x.shape
    c = wout_bf.shape[0]
    hist = CONV_WIDTH - 1
    kern = functools.partial(_conv_mixer_kernel, bb=bb, tl=tl, bbk=bbk, rb=rb, cbw=128)
    return pl.pallas_call(
        kern,
        grid=(b // bb, l // tl),
        in_specs=[pl.BlockSpec((bb, tl, d), lambda i, j: (i, j, 0)),
                  _resident((1, d)),
                  pl.BlockSpec((bb, hist, c), lambda i, j: (i, 0, 0)),
                  _resident((d, 2 * c)),
                  _resident((CONV_WIDTH, SUBLANES, c)),
                  _resident((1, c)),
                  _resident((1, c)),
                  _resident((1, c)),
                  _resident((c, d))],
        out_specs=[pl.BlockSpec((bb, tl, d), lambda i, j: (i, j, 0)),
                   pl.BlockSpec((bb, hist, c), lambda i, j: (i, 0, 0))],
        out_shape=[jax.ShapeDtypeStruct((b, l, d), F32),
                   jax.ShapeDtypeStruct((b, hist, c), F32)],
        scratch_shapes=[pltpu.VMEM((bb, 4 * SUBLANES + tl, c), F32),
                        pltpu.VMEM((bb, tl, c), F32),
                        pltpu.VMEM((bb, tl, c), _mm_dtype(rb))],
        compiler_params=_params("parallel", "arbitrary"),
        name="conv_mixer",
    )(x, g.reshape(1, d), state, win_bf, jnp.broadcast_to(wdw[:, None, :], (CONV_WIDTH, SUBLANES, c)),
      bdw.reshape(1, c), lng.reshape(1, c), lnb.reshape(1, c), wout_bf)


def _gla_gate_kernel(g1_ref, w2_ref, b_ref, o_ref):
    z = _dot(g1_ref[...].astype(BF16), w2_ref[...]) + b_ref[...]
    o_ref[...] = _log_sigmoid(z) * (1.0 / GLA_GATE_NORM)


def gla_gate(h2, w2_pad_bf, b):
    m = h2.shape[0]
    dk = w2_pad_bf.shape[1]
    tm = min(m, 1024)
    g1_block = (2 * dk + 2 * GLA_HEADS * GLA_HEAD_V) // LANES
    return pl.pallas_call(
        _gla_gate_kernel,
        grid=(m // tm,),
        in_specs=[pl.BlockSpec((tm, LANES), lambda i: (i, g1_block)),
                  _resident((LANES, dk)),
                  _resident((1, dk))],
        out_specs=pl.BlockSpec((tm, dk), lambda i: (i, 0)),
        out_shape=jax.ShapeDtypeStruct((m, dk), F32),
        compiler_params=_params("parallel"),
        name="gla_gate",
    )(h2, w2_pad_bf, b.reshape(1, dk))


def _gla_rec_kernel(q_ref, k_ref, v_ref, g_ref, s0_ref, o_ref, s_ref, st_sc, *, tl, chunk):
    l = pl.program_id(1)
    mm = _mm_dtype(chunk)
    nh, hk, hv = GLA_HEADS, GLA_HEAD_K, GLA_HEAD_V

    @pl.when(l == 0)
    def _():
        for h in range(nh):
            st_sc[h] = s0_ref[0, h].T

    row = lax.broadcasted_iota(jnp.int32, (tl, tl), 0)
    col = lax.broadcasted_iota(jnp.int32, (tl, tl), 1)
    tril = ((row // chunk == col // chunk) & (col <= row)).astype(mm)
    gcum_all = _dot_exact_lhs01(tril, g_ref[0], mm)

    crow = lax.broadcasted_iota(jnp.int32, (chunk, chunk), 0)
    ccol = lax.broadcasted_iota(jnp.int32, (chunk, chunk), 1)
    causal = ccol <= crow

    for h in range(nh):
        kcols = slice(h * hk, (h + 1) * hk)
        vcols = slice(h * hv, (h + 1) * hv)
        q = q_ref[0, :, kcols] * (hk ** -0.5)
        k = k_ref[0, :, kcols]
        v = v_ref[0, :, vcols]
        gcum = gcum_all[:, kcols]
        st = st_sc[h]
        outs = []
        for c in range(tl // chunk):
            sl = slice(c * chunk, (c + 1) * chunk)
            gc = gcum[sl]
            gl = gc[chunk - 1:chunk]
            qe = (q[sl] * jnp.exp(gc)).astype(mm)
            ke = (k[sl] * jnp.exp(-gc)).astype(mm)
            kd = (k[sl] * jnp.exp(gl - gc)).astype(mm)
            vc = v[sl].astype(mm)
            scores = jnp.where(causal, _dot_nt(qe, ke), 0.0)
            outs.append(_dot_nt(qe, st.astype(mm)) + _dot(scores.astype(mm), vc))
            st = st * jnp.exp(gl) + _dot_tn(vc, kd)
        o_ref[0, :, vcols] = outs[0] if len(outs) == 1 else jnp.concatenate(outs, axis=0)
        st_sc[h] = st

    @pl.when(l == pl.num_programs(1) - 1)
    def _():
        for h in range(nh):
            s_ref[0, h] = st_sc[h].T


def gla_recurrence(h3, gk3, s0, *, tl, chunk):
    b, l, _ = h3.shape
    hk, hv, nh = GLA_HEAD_K, GLA_HEAD_V, GLA_HEADS
    dk, dv = nh * hk, nh * hv
    kern = functools.partial(_gla_rec_kernel, tl=tl, chunk=chunk)
    return pl.pallas_call(
        kern,
        grid=(b, l // tl),
        in_specs=[pl.BlockSpec((1, tl, dk), lambda i, j: (i, j, 0)),
                  pl.BlockSpec((1, tl, dk), lambda i, j: (i, j, 1)),
                  pl.BlockSpec((1, tl, dv), lambda i, j: (i, j, (2 * dk) // dv)),
                  pl.BlockSpec((1, tl, dk), lambda i, j: (i, j, 0)),
                  pl.BlockSpec((1, nh, hk, hv), lambda i, j: (i, 0, 0, 0))],
        out_specs=[pl.BlockSpec((1, tl, dv), lambda i, j: (i, j, 0)),
                   pl.BlockSpec((1, nh, hk, hv), lambda i, j: (i, 0, 0, 0))],
        out_shape=[jax.ShapeDtypeStruct((b, l, dv), F32),
                   jax.ShapeDtypeStruct((b, nh, hk, hv), F32)],
        scratch_shapes=[pltpu.VMEM((nh, hv, hk), F32)],
        compiler_params=_params("parallel", "arbitrary"),
        name="gla_recurrence",
    )(h3, h3, h3, gk3, s0)


def _gla_out_kernel(x_ref, o_ref, r_ref, gn_ref, w_ref, y_ref):
    o = o_ref[...]
    gn = gn_ref[...]
    parts = []
    for h in range(GLA_HEADS):
        parts.append(_rms(o[:, h * GLA_HEAD_V:(h + 1) * GLA_HEAD_V], gn))
    on = jnp.concatenate(parts, axis=-1)
    r = r_ref[...]
    gated = (on * (r * jax.nn.sigmoid(r))).astype(BF16)
    y_ref[...] = x_ref[...] + _dot(gated, w_ref[...])


def gla_out(x2, o2, h2, gn, wout_bf):
    m, d = x2.shape
    dv = o2.shape[1]
    tm = min(m, 512)
    r_block = (2 * GLA_HEADS * GLA_HEAD_K + dv) // dv
    return pl.pallas_call(
        _gla_out_kernel,
        grid=(m // tm,),
        in_specs=[pl.BlockSpec((tm, d), lambda i: (i, 0)),
                  pl.BlockSpec((tm, dv), lambda i: (i, 0)),
                  pl.BlockSpec((tm, dv), lambda i: (i, r_block)),
                  _resident((1, GLA_HEAD_V)),
                  _resident((dv, d))],
        out_specs=pl.BlockSpec((tm, d), lambda i: (i, 0)),
        out_shape=jax.ShapeDtypeStruct((m, d), F32),
        compiler_params=_params("parallel"),
        name="gla_out",
    )(x2, o2, h2, gn.reshape(1, GLA_HEAD_V), wout_bf)


def _head_norm(x, g2):
    lo = lax.broadcasted_iota(jnp.int32, (1, LANES), 1) < FOX_HEAD_DIM
    parts = []
    for j in range(x.shape[-1] // LANES):
        xb = x[:, j * LANES:(j + 1) * LANES]
        sq = xb * xb
        s_lo = jnp.sum(jnp.where(lo, sq, 0.0), axis=-1, keepdims=True)
        s_hi = jnp.sum(jnp.where(lo, 0.0, sq), axis=-1, keepdims=True)
        ms = jnp.where(lo, s_lo, s_hi) * (1.0 / FOX_HEAD_DIM)
        parts.append(xb * lax.rsqrt(ms + EPS) * g2)
    return jnp.concatenate(parts, axis=-1)


def _aug_select():
    sq = np.zeros((4 * LANES, FOX_HEADS * LANES), np.float32)
    sk = np.zeros((4 * LANES, FOX_HEADS * LANES), np.float32)
    for h in range(FOX_HEADS):
        a0 = h * LANES + (FOX_HEAD_DIM if h % 2 == 0 else 0)
        for i in range(3):
            sq[i * LANES + h, a0 + i] = 1.0
            sq[3 * LANES, a0 + 3 + i] = 1.0
            sk[3 * LANES, a0 + i] = 1.0
            sk[i * LANES + h, a0 + 3 + i] = -1.0
    return sq, sk


def _fox_prep_prompt_kernel(q_ref, k_ref, v_ref, fl_ref, qg_ref, kg_ref, bf_ref, sq_ref, sk_ref,
                            qa_ref, kn_ref, ka_ref, vo_ref, va_ref, lf_ref, carry_sc, *, tl):
    l = pl.program_id(1)

    @pl.when(l == 0)
    def _():
        carry_sc[...] = jnp.zeros_like(carry_sc)

    qn = _head_norm(q_ref[0], qg_ref[...]) * (FOX_HEAD_DIM ** -0.5 * LOG2E)
    kn = _head_norm(k_ref[0], kg_ref[...])
    v = v_ref[0]
    kn_ref[0] = kn
    vo_ref[0] = v

    lf = _log_sigmoid(fl_ref[0] + bf_ref[...])
    lf_ref[0] = lf[:, :FOX_HEADS]
    row = lax.broadcasted_iota(jnp.int32, (tl, tl), 0)
    col = lax.broadcasted_iota(jnp.int32, (tl, tl), 1)
    c = _dot_exact_lhs01((col <= row).astype(BF16), lf, BF16) + carry_sc[...]
    carry_sc[...] = c[tl - 1:tl]
    pieces = jnp.concatenate(list(_split3(c * LOG2E)) + [jnp.ones((tl, LANES), F32)], axis=1).astype(BF16)
    qaug = _dot(pieces, sq_ref[...])
    kaug = _dot(pieces, sk_ref[...])

    lane = lax.broadcasted_iota(jnp.int32, (1, LANES), 1)
    lo = lane < FOX_HEAD_DIM
    for h in range(FOX_HEADS):
        src = slice((h // 2) * LANES, (h // 2 + 1) * LANES)
        dst = slice(h * LANES, (h + 1) * LANES)
        own = lo if h % 2 == 0 else jnp.logical_not(lo)
        vaug = jnp.where(lane == (FOX_HEAD_DIM if h % 2 == 0 else 0), 1.0, 0.0)
        qa_ref[0, :, dst] = jnp.where(own, qn[:, src], qaug[:, dst]).astype(BF16)
        ka_ref[0, :, dst] = jnp.where(own, kn[:, src], kaug[:, dst]).astype(BF16)
        va_ref[0, :, dst] = jnp.where(own, v[:, src], vaug).astype(BF16)


def fox_prep_prompt(h3, qg, kg, bf, *, tl):
    b, l, _ = h3.shape
    d = FOX_HEADS * FOX_HEAD_DIM
    da = FOX_HEADS * LANES
    fl_block = (4 * d) // LANES
    qg2 = jnp.tile(qg, 2).reshape(1, LANES)
    kg2 = jnp.tile(kg, 2).reshape(1, LANES)
    bf_pad = jnp.pad(bf, (0, LANES - FOX_HEADS)).reshape(1, LANES)
    sq, sk = _aug_select()
    row3 = lambda i, j: (i, j, 0)
    kern = functools.partial(_fox_prep_prompt_kernel, tl=tl)
    return pl.pallas_call(
        kern,
        grid=(b, l // tl),
        in_specs=[pl.BlockSpec((1, tl, d), lambda i, j: (i, j, 0)),
                  pl.BlockSpec((1, tl, d), lambda i, j: (i, j, 1)),
                  pl.BlockSpec((1, tl, d), lambda i, j: (i, j, 2)),
                  pl.BlockSpec((1, tl, LANES), lambda i, j: (i, j, fl_block)),
                  _resident((1, LANES)), _resident((1, LANES)), _resident((1, LANES)),
                  _resident(sq.shape), _resident(sk.shape)],
        out_specs=[pl.BlockSpec((1, tl, da), row3),
                   pl.BlockSpec((1, tl, d), row3),
                   pl.BlockSpec((1, tl, da), row3),
                   pl.BlockSpec((1, tl, d), row3),
                   pl.BlockSpec((1, tl, da), row3),
                   pl.BlockSpec((1, tl, FOX_HEADS), row3)],
        out_shape=[jax.ShapeDtypeStruct((b, l, da), BF16),
                   jax.ShapeDtypeStruct((b, l, d), F32),
                   jax.ShapeDtypeStruct((b, l, da), BF16),
                   jax.ShapeDtypeStruct((b, l, d), F32),
                   jax.ShapeDtypeStruct((b, l, da), BF16),
                   jax.ShapeDtypeStruct((b, l, FOX_HEADS), F32)],
        scratch_shapes=[pltpu.VMEM((1, LANES), F32)],
        compiler_params=_params("parallel", "arbitrary"),
        name="fox_prep_prompt",
    )(h3, h3, h3, h3, qg2, kg2, bf_pad, jnp.asarray(sq, BF16), jnp.asarray(sk, BF16))


def _fox_prep_sample_kernel(q_ref, k_ref, v_ref, fl_ref, qg_ref, kg_ref, bf_ref, qo_ref, kn_ref, vo_ref, lf_ref):
    qo_ref[...] = _head_norm(q_ref[...], qg_ref[...]) * (FOX_HEAD_DIM ** -0.5)
    kn_ref[...] = _head_norm(k_ref[...], kg_ref[...])
    vo_ref[...] = v_ref[...]
    lf_ref[...] = _log_sigmoid(fl_ref[...] + bf_ref[...])[:, :FOX_HEADS]


def fox_prep_sample(h2, qg, kg, bf):
    m = h2.shape[0]
    d = FOX_HEADS * FOX_HEAD_DIM
    fl_block = (4 * d) // LANES
    qg2 = jnp.tile(qg, 2).reshape(1, LANES)
    kg2 = jnp.tile(kg, 2).reshape(1, LANES)
    bf_pad = jnp.pad(bf, (0, LANES - FOX_HEADS)).reshape(1, LANES)
    full = lambda i: (0, 0)
    return pl.pallas_call(
        _fox_prep_sample_kernel,
        grid=(1,),
        in_specs=[pl.BlockSpec((m, d), lambda i: (0, 0)),
                  pl.BlockSpec((m, d), lambda i: (0, 1)),
                  pl.BlockSpec((m, d), lambda i: (0, 2)),
                  pl.BlockSpec((m, LANES), lambda i: (0, fl_block)),
                  _resident((1, LANES)), _resident((1, LANES)), _resident((1, LANES))],
        out_specs=[pl.BlockSpec((m, d), full), pl.BlockSpec((m, d), full), pl.BlockSpec((m, d), full),
                   pl.BlockSpec((m, FOX_HEADS), full)],
        out_shape=[jax.ShapeDtypeStruct((m, d), F32), jax.ShapeDtypeStruct((m, d), F32),
                   jax.ShapeDtypeStruct((m, d), F32), jax.ShapeDtypeStruct((m, FOX_HEADS), F32)],
        compiler_params=_params("arbitrary"),
        name="fox_prep_sample",
    )(h2, h2, h2, h2, qg2, kg2, bf_pad)


def _flash_kernel(qt_ref, kt_ref, q_ref, k_ref, v_ref, o_ref, m_sc, acc_sc, *, t):
    step = pl.program_id(1)
    qi = qt_ref[step]
    ki = kt_ref[step]
    lo = lax.broadcasted_iota(jnp.int32, (1, LANES), 1) < FOX_HEAD_DIM

    @pl.when(ki == 0)
    def _():
        m_sc[...] = jnp.full_like(m_sc, -jnp.inf)
        acc_sc[...] = jnp.zeros_like(acc_sc)

    def head_update(h, diagonal):
        cols = pl.ds(pl.multiple_of(h * LANES, LANES), LANES)
        s = _dot_nt(q_ref[0, :, cols], k_ref[0, :, cols])
        if diagonal:
            row = lax.broadcasted_iota(jnp.int32, (t, t), 0)
            col = lax.broadcasted_iota(jnp.int32, (t, t), 1)
            s = jnp.where(col <= row, s, -jnp.inf)
        m_prev = m_sc[h]
        m_new = jnp.maximum(m_prev, jnp.max(s, axis=-1, keepdims=True))
        alpha = jnp.exp2(m_prev - m_new)
        p = jnp.exp2(s - jnp.concatenate([m_new] * (t // LANES), axis=1))
        acc_sc[h] = acc_sc[h] * alpha + _dot(p.astype(BF16), v_ref[0, :, cols])
        m_sc[h] = m_new

    @pl.when(ki < qi)
    def _():
        def body(j, carry):
            head_update(2 * j, False)
            head_update(2 * j + 1, False)
            return carry

        lax.fori_loop(0, FOX_HEADS // 2, body, 0, unroll=4)

    @pl.when(ki == qi)
    def _():
        def body(j, carry):
            head_update(2 * j, True)
            head_update(2 * j + 1, True)
            a0 = acc_sc[2 * j]
            a1 = acc_sc[2 * j + 1]
            out = jnp.where(lo, a0 / a0[:, FOX_HEAD_DIM:FOX_HEAD_DIM + 1], a1 / a1[:, 0:1])
            o_ref[0, :, pl.ds(pl.multiple_of(j * LANES, LANES), LANES)] = out
            return carry

        lax.fori_loop(0, FOX_HEADS // 2, body, 0)


def fox_flash(qa, ka, va, *, t):
    b, l, da = qa.shape
    d = FOX_HEADS * FOX_HEAD_DIM
    n = l // t
    qi = np.concatenate([np.full(i + 1, i, np.int32) for i in range(n)])
    ki = np.concatenate([np.arange(i + 1, dtype=np.int32) for i in range(n)])
    kern = functools.partial(_flash_kernel, t=t)
    grid_spec = pltpu.PrefetchScalarGridSpec(
        num_scalar_prefetch=2,
        grid=(b, len(qi)),
        in_specs=[pl.BlockSpec((1, t, da), lambda i, s, qt, kt: (i, qt[s], 0)),
                  pl.BlockSpec((1, t, da), lambda i, s, qt, kt: (i, kt[s], 0)),
                  pl.BlockSpec((1, t, da), lambda i, s, qt, kt: (i, kt[s], 0))],
        out_specs=pl.BlockSpec((1, t, d), lambda i, s, qt, kt: (i, qt[s], 0)),
        scratch_shapes=[pltpu.VMEM((FOX_HEADS, t, LANES), F32),
                        pltpu.VMEM((FOX_HEADS, t, LANES), F32)])
    return pl.pallas_call(
        kern,
        grid_spec=grid_spec,
        out_shape=jax.ShapeDtypeStruct((b, l, d), F32),
        compiler_params=_params("parallel", "arbitrary"),
        name="fox_flash",
    )(jnp.asarray(qi), jnp.asarray(ki), qa, ka, va)


def _decode_kernel(pt_ref, q_ref, kn_ref, vn_ref, lfn_ref, *refs, lq, page, pg):
    del pt_ref
    kt_refs, vt_refs, lft_refs = refs[:pg], refs[pg:2 * pg], refs[2 * pg:3 * pg]
    o_ref = refs[3 * pg]
    qbd_sc, m_sc, l_sc, acc_sc, cq_sc, carry_sc = refs[3 * pg + 1:]
    p = pl.program_id(1)
    nh, hd = FOX_HEADS, FOX_HEAD_DIM
    rows = nh * lq
    d = nh * hd

    uj = lax.broadcasted_iota(jnp.int32, (page, 2 * page), 0)
    us = lax.broadcasted_iota(jnp.int32, (page, 2 * page), 1)
    u2 = ((uj > us) | (us >= page)).astype(BF16)

    def later_sums(x, keep):
        out = jnp.zeros((x.shape[0], 2 * page), F32)
        for piece in _split3(x):
            if keep is not None:
                piece = jnp.where(keep, piece, 0.0)
            out = out + _dot(piece.astype(BF16), u2)
        return out

    def head_rows(x):
        return jnp.concatenate([jnp.broadcast_to(x[h:h + 1], (lq, x.shape[1])) for h in range(nh)], axis=0)

    @pl.when(p == 0)
    def _():
        qt = jnp.concatenate([q_ref[0]] * nh, axis=0)
        rr = lax.broadcasted_iota(jnp.int32, (rows, d), 0)
        cc = lax.broadcasted_iota(jnp.int32, (rows, d), 1)
        qbd_sc[...] = jnp.where(rr // lq == cc // hd, qt, 0.0)

        pad = page - lq
        kn = jnp.concatenate([kn_ref[0], jnp.zeros((pad, d), F32)], axis=0)
        vn = jnp.concatenate([vn_ref[0], jnp.zeros((pad, d), F32)], axis=0)
        lfn = jnp.concatenate([lfn_ref[0], jnp.zeros((pad, nh), F32)], axis=0)
        er = lax.broadcasted_iota(jnp.int32, (rows, nh), 0)
        ec = lax.broadcasted_iota(jnp.int32, (rows, nh), 1)
        expand = (er // lq == ec).astype(BF16)
        t = lax.broadcasted_iota(jnp.int32, (rows, page), 0) % lq
        j = lax.broadcasted_iota(jnp.int32, (rows, page), 1)
        sums = jnp.zeros((rows, 2 * page), F32)
        for piece in _split3(lfn):
            e = jnp.where(j <= t, _dot_nt(expand, piece.astype(BF16)), 0.0)
            sums = sums + _dot(e.astype(BF16), u2)
        cq_sc[...] = sums[:, page:]
        s = jnp.where(j <= t, _dot_nt(qbd_sc[...], kn) + sums[:, :page], -jnp.inf)
        m_cur = jnp.max(s, axis=-1, keepdims=True)
        pexp = jnp.exp(s - m_cur)
        m_sc[...] = jnp.broadcast_to(m_cur, (rows, LANES))
        l_sc[...] = jnp.broadcast_to(jnp.sum(pexp, axis=-1, keepdims=True), (rows, LANES))
        acc_sc[...] = _dot(pexp, vn)
        carry_sc[...] = jnp.zeros_like(carry_sc)

    lft = jnp.concatenate([r[0] for r in lft_refs], axis=0)
    sums = later_sums(lft, None)
    carry = carry_sc[...]
    tiles = []
    for i in range(pg):
        blk = slice(i * nh, (i + 1) * nh)
        bias = head_rows(sums[blk, :page] + carry) + cq_sc[...]
        carry = carry + sums[blk, page:]
        tiles.append(_dot(qbd_sc[...], kt_refs[i][0]) + bias)
    carry_sc[...] = carry
    s = jnp.concatenate(tiles, axis=1)
    m_prev = m_sc[...]
    m_new = jnp.maximum(m_prev, jnp.max(s, axis=-1, keepdims=True))
    alpha = jnp.exp(m_prev - m_new)
    pexp = jnp.exp(s - jnp.concatenate([m_new] * pg, axis=1))
    l_sc[...] = alpha * l_sc[...] + jnp.sum(pexp, axis=-1, keepdims=True)
    m_sc[...] = m_new
    pv = _dot_nt(pexp[:, :page], vt_refs[0][0])
    for i in range(1, pg):
        pv = pv + _dot_nt(pexp[:, i * page:(i + 1) * page], vt_refs[i][0])
    for c in range(d // LANES):
        cols = slice(c * LANES, (c + 1) * LANES)
        acc_sc[:, cols] = acc_sc[:, cols] * alpha + pv[:, cols]

    @pl.when(p == pl.num_programs(1) - 1)
    def _():
        rr = lax.broadcasted_iota(jnp.int32, (rows, d), 0)
        cc = lax.broadcasted_iota(jnp.int32, (rows, d), 1)
        inv = 1.0 / l_sc[...]
        parts = [acc_sc[:, c * LANES:(c + 1) * LANES] * inv for c in range(d // LANES)]
        on = jnp.where(rr // lq == cc // hd, jnp.concatenate(parts, axis=-1), 0.0)
        o_ref[0] = jnp.sum(on.reshape(nh, lq, d), axis=0)


def fox_decode(q3, kn3, vn3, lfn3, kpool_t, vpool_t, lfpool_t, page_table, *, pg):
    b, lq, d = q3.shape
    nh = FOX_HEADS
    n_pages = page_table.shape[1]
    page = lfpool_t.shape[2]
    rows = nh * lq
    assert page == LANES and rows == LANES and n_pages % pg == 0

    def new_map(i, p, pt):
        return (i, 0, 0)

    def page_map(k):
        return lambda i, p, pt: (pt[i, n_pages - 1 - (p * pg + k)], 0, 0)

    kern = functools.partial(_decode_kernel, lq=lq, page=page, pg=pg)
    grid_spec = pltpu.PrefetchScalarGridSpec(
        num_scalar_prefetch=1,
        grid=(b, n_pages // pg),
        in_specs=[pl.BlockSpec((1, lq, d), new_map),
                  pl.BlockSpec((1, lq, d), new_map),
                  pl.BlockSpec((1, lq, d), new_map),
                  pl.BlockSpec((1, lq, nh), new_map)]
                 + [pl.BlockSpec((1, d, page), page_map(k)) for k in range(pg)]
                 + [pl.BlockSpec((1, d, page), page_map(k)) for k in range(pg)]
                 + [pl.BlockSpec((1, nh, page), page_map(k)) for k in range(pg)],
        out_specs=pl.BlockSpec((1, lq, d), new_map),
        scratch_shapes=[pltpu.VMEM((rows, d), F32),
                        pltpu.VMEM((rows, LANES), F32),
                        pltpu.VMEM((rows, LANES), F32),
                        pltpu.VMEM((rows, d), F32),
                        pltpu.VMEM((rows, LANES), F32),
                        pltpu.VMEM((nh, LANES), F32)])
    return pl.pallas_call(
        kern,
        grid_spec=grid_spec,
        out_shape=jax.ShapeDtypeStruct((b, lq, d), F32),
        compiler_params=_params("parallel", "arbitrary"),
        name="fox_decode",
    )(page_table, q3, kn3, vn3, lfn3, *([kpool_t] * pg), *([vpool_t] * pg), *([lfpool_t] * pg))


def _fox_out_kernel(x_ref, o_ref, og_ref, w_ref, y_ref):
    gated = (o_ref[...] * jax.nn.sigmoid(og_ref[...])).astype(BF16)
    y_ref[...] = x_ref[...] + _dot(gated, w_ref[...])


def fox_out(x2, o2, h2, wout_bf):
    m, d = x2.shape
    tm = min(m, 512)
    return pl.pallas_call(
        _fox_out_kernel,
        grid=(m // tm,),
        in_specs=[pl.BlockSpec((tm, d), lambda i: (i, 0)),
                  pl.BlockSpec((tm, d), lambda i: (i, 0)),
                  pl.BlockSpec((tm, d), lambda i: (i, 3)),
                  _resident((d, d))],
        out_specs=pl.BlockSpec((tm, d), lambda i: (i, 0)),
        out_shape=jax.ShapeDtypeStruct((m, d), F32),
        compiler_params=_params("parallel"),
        name="fox_out",
    )(x2, o2, h2, wout_bf)


def _tiles(x):
    b, l, _ = x.shape
    if l >= 512:
        return 1, 512
    return b, l


def _conv_layer(x, g, state, prm):
    bb, tl = _tiles(x)
    tl = min(tl, 256)
    bbk, rb = (1, 64) if tl >= 64 else (64 // tl, tl)
    return conv_mixer(x, g, state, *prm, bb=bb, tl=tl, bbk=bbk, rb=rb)


def _ffn_layer(x, g, state, prm):
    bb, tl = _tiles(x)
    return conv_ffn(x, g, state, *prm, bb=bb, tl=tl, cw=FFN_CHUNK)


def _gla_layer(x, g, s0, prm):
    win_bf, w2_bf, b_gk, gn, wout_bf = prm
    b, l, d = x.shape
    x2 = x.reshape(b * l, d)
    h2 = norm_matmul(x2, g, win_bf, 640)
    gk2 = gla_gate(h2, w2_bf, b_gk)
    chunk = GLA_CHUNK if l % GLA_CHUNK == 0 else l
    tl = 4 * chunk if l % (4 * chunk) == 0 else chunk
    o3, s = gla_recurrence(h2.reshape(b, l, -1), gk2.reshape(b, l, -1), s0, tl=tl, chunk=chunk)
    y2 = gla_out(x2, o3.reshape(b * l, -1), h2, gn, wout_bf)
    return y2.reshape(b, l, d), s


def _fox_layer(x, g, cache, prm):
    win_bf, b_f, qg, kg, wout_bf = prm
    b, l, d = x.shape
    x2 = x.reshape(b * l, d)
    h2 = norm_matmul(x2, g, win_bf, 1408)
    if cache is None:
        qa, kn, ka, vo, va, lf = fox_prep_prompt(h2.reshape(b, l, -1), qg, kg, b_f, tl=512)
        o3 = fox_flash(qa, ka, va, t=512)
    else:
        kpool_t, vpool_t, lfpool_t, page_table = cache
        q2, kn, vo, lf = fox_prep_sample(h2, qg, kg, b_f)
        o3 = fox_decode(q2.reshape(b, l, d), kn.reshape(b, l, d), vo.reshape(b, l, d), lf.reshape(b, l, FOX_HEADS),
                        kpool_t, vpool_t, lfpool_t, page_table, pg=8)
    y2 = fox_out(x2, o3.reshape(b * l, d), h2, wout_bf)
    shp = (b, l, FOX_HEADS, FOX_HEAD_DIM)
    return y2.reshape(b, l, d), kn.reshape(shp), vo.reshape(shp), lf.reshape(b, l, FOX_HEADS)


def _pad_cols(w, n):
    return jnp.pad(w, ((0, 0), (0, n - w.shape[1])))


def kernel(x_prompt, x_sample, state_conv, state_gla, cache_fox_k, cache_fox_v, cache_fox_logf, state_ffn_conv, page_table,
           norm_mix_g, norm_ffn_g, w_conv_in, w_conv_dw, b_conv_dw, conv_ln_g, conv_ln_b, w_conv_out,
           w_gla_in, w_gla_gk2, b_gla_gk, gla_norm_g, w_gla_out,
           w_fox_in, b_fox_f, fox_qn_g, fox_kn_g, w_fox_out,
           w_ffn_up, w_ffn_dw, b_ffn_dw, w_ffn_down):
    depth = norm_mix_g.shape[0]
    bp = x_prompt.shape[0]
    d = x_prompt.shape[-1]
    xp, xs = x_prompt, x_sample
    conv_p, conv_s, gla_p, gla_s = [], [], [], []
    fk_p, fk_s, fv_p, fv_s, fl_p, fl_s = [], [], [], [], [], []
    ffn_p, ffn_s = [], []
    for i in range(depth):
        m, j = i % 3, i // 3
        if m == 0:
            prm = (w_conv_in[j].astype(BF16), w_conv_dw[j], b_conv_dw[j], conv_ln_g[j], conv_ln_b[j],
                   w_conv_out[j].astype(BF16))
            zero = jnp.zeros((bp,) + state_conv.shape[2:], F32)
            xp, st_p = _conv_layer(xp, norm_mix_g[i], zero, prm)
            xs, st_s = _conv_layer(xs, norm_mix_g[i], state_conv[j], prm)
            conv_p.append(st_p)
            conv_s.append(st_s)
        elif m == 1:
            n_in = w_gla_in.shape[-1]
            n_pad = -(-n_in // (5 * LANES)) * (5 * LANES)
            w2 = jnp.pad(w_gla_gk2[j], ((0, LANES - GLA_GATE_RANK), (0, 0))).astype(BF16)
            prm = (_pad_cols(w_gla_in[j], n_pad).astype(BF16), w2, b_gla_gk[j], gla_norm_g[j],
                   w_gla_out[j].astype(BF16))
            zero = jnp.zeros((bp,) + state_gla.shape[2:], F32)
            xp, st_p = _gla_layer(xp, norm_mix_g[i], zero, prm)
            xs, st_s = _gla_layer(xs, norm_mix_g[i], state_gla[j], prm)
            gla_p.append(st_p)
            gla_s.append(st_s)
        else:
            n_pad = 4 * d + LANES
            prm = (_pad_cols(w_fox_in[j], n_pad).astype(BF16), b_fox_f[j], fox_qn_g[j], fox_kn_g[j],
                   w_fox_out[j].astype(BF16))
            n_pool, page = cache_fox_k.shape[1], cache_fox_k.shape[2]
            cache = (jnp.transpose(cache_fox_k[j], (0, 2, 3, 1)).reshape(n_pool, d, page),
                     jnp.transpose(cache_fox_v[j], (0, 2, 3, 1)).reshape(n_pool, d, page),
                     jnp.transpose(cache_fox_logf[j], (0, 2, 1)), page_table)
            xp, kp, vp, lp = _fox_layer(xp, norm_mix_g[i], None, prm)
            xs, kn, vn, ln = _fox_layer(xs, norm_mix_g[i], cache, prm)
            fk_p.append(kp)
            fk_s.append(kn)
            fv_p.append(vp)
            fv_s.append(vn)
            fl_p.append(lp)
            fl_s.append(ln)
        fprm = (w_ffn_up[i].astype(BF16), w_ffn_dw[i], b_ffn_dw[i], w_ffn_down[i].astype(BF16))
        zero = jnp.zeros((bp,) + state_ffn_conv.shape[2:], F32)
        xp, st_p = _ffn_layer(xp, norm_ffn_g[i], zero, fprm)
        xs, st_s = _ffn_layer(xs, norm_ffn_g[i], state_ffn_conv[i], fprm)
        ffn_p.append(st_p)
        ffn_s.append(st_s)
    return (xp, xs, jnp.stack(conv_p), jnp.stack(conv_s), jnp.stack(gla_p), jnp.stack(gla_s),
            jnp.stack(fk_p), jnp.stack(fk_s), jnp.stack(fv_p), jnp.stack(fv_s), jnp.stack(fl_p), jnp.stack(fl_s),
            jnp.stack(ffn_p), jnp.stack(ffn_s))
```

```python
import functools

import jax
import jax.numpy as jnp
import numpy as np
from jax import lax
from jax.experimental import pallas as pl
from jax.experimental.pallas import tpu as pltpu

F32 = jnp.float32
BF16 = jnp.bfloat16
EPS = 1e-6
LOG2E = 1.4426950408889634

LANES = 128
SUBLANES = 8
VMEM_LIMIT_BYTES = 58 * 1024 * 1024

GLA_HEADS = 4
GLA_HEAD_K = 128
GLA_HEAD_V = 256
GLA_GATE_RANK = 16
GLA_GATE_NORM = 16.0
GLA_CHUNK = 64
FOX_HEADS = 16
FOX_HEAD_DIM = 64
CONV_WIDTH = 31
FFN_CONV_WIDTH = 3
FFN_CHUNK = 512


def _params(*sem):
    return pltpu.CompilerParams(dimension_semantics=sem, vmem_limit_bytes=VMEM_LIMIT_BYTES)


def _resident(shape):
    nd = len(shape)
    return pl.BlockSpec(shape, lambda *_: (0,) * nd, pipeline_mode=pl.Buffered(1))


def _dot(a, b):
    return jnp.dot(a, b, preferred_element_type=F32)


def _dot_nt(a, b):
    return lax.dot_general(a, b, (((1,), (1,)), ((), ())), preferred_element_type=F32)


def _dot_tn(a, b):
    return lax.dot_general(a, b, (((0,), (0,)), ((), ())), preferred_element_type=F32)


def _split3(x):
    hi = x.astype(BF16).astype(F32)
    r = x - hi
    mid = r.astype(BF16).astype(F32)
    lo = (r - mid).astype(BF16).astype(F32)
    return hi, mid, lo


def _dot_exact_lhs01(t, x, mm):
    hi, mid, lo = _split3(x)
    return _dot(t, hi.astype(mm)) + _dot(t, mid.astype(mm)) + _dot(t, lo.astype(mm))


def _rms(x, g):
    return x * lax.rsqrt(jnp.mean(x * x, axis=-1, keepdims=True) + EPS) * g


def _log_sigmoid(z):
    return jnp.minimum(z, 0.0) - jnp.log1p(jnp.exp(-jnp.abs(z)))


def _mm_dtype(rows):
    return BF16 if rows % 16 == 0 else F32


def _norm_matmul_kernel(x_ref, g_ref, w_ref, o_ref):
    o_ref[...] = _dot(_rms(x_ref[...], g_ref[...]).astype(BF16), w_ref[...])


def norm_matmul(x2, g, w_bf):
    m, d = x2.shape
    n = w_bf.shape[1]
    tm = min(m, 512)
    return pl.pallas_call(
        _norm_matmul_kernel,
        grid=(m // tm,),
        in_specs=[pl.BlockSpec((tm, d), lambda i: (i, 0)),
                  _resident((1, d)),
                  _resident((d, n))],
        out_specs=pl.BlockSpec((tm, n), lambda i: (i, 0)),
        out_shape=jax.ShapeDtypeStruct((m, n), F32),
        compiler_params=_params("parallel"),
        name="norm_matmul",
    )(x2, g.reshape(1, d), w_bf)


def _ffn_kernel(x_ref, g_ref, st_ref, wup_ref, wdw_ref, bdw_ref, wdn_ref, y_ref, nst_ref,
                u_sc, act_sc, *, bb, tl, f, cw):
    l = pl.program_id(1)
    m = bb * tl
    d = x_ref.shape[-1]
    pre = SUBLANES - (FFN_CONV_WIDTH - 1)

    @pl.when(l == 0)
    def _():
        u_sc[:, pre:SUBLANES, :] = st_ref[...]

    x = x_ref[...].reshape(m, d)
    hn = _rms(x, g_ref[...]).astype(BF16)
    for c0 in range(0, f, cw):
        cn = min(cw, f - c0)
        halves = []
        for col in (c0, f + c0):
            cols = slice(col, col + cn)
            u_sc[:, SUBLANES:, cols] = _dot(hn, wup_ref[:, cols]).reshape(bb, tl, cn)
            acc = bdw_ref[:, cols]
            for w in range(FFN_CONV_WIDTH):
                acc = acc + wdw_ref[w:w + 1, cols] * u_sc[:, pre + w:pre + w + tl, cols]
            halves.append(acc)
        ya, yb = halves
        act_sc[:, c0:c0 + cn] = (ya * jax.nn.sigmoid(ya) * yb).reshape(m, cn).astype(BF16)
    y_ref[...] = (x + _dot(act_sc[...], wdn_ref[...])).reshape(bb, tl, d)

    new = u_sc[:, tl + pre:tl + SUBLANES, :]

    @pl.when(l == pl.num_programs(1) - 1)
    def _():
        nst_ref[...] = new

    u_sc[:, pre:SUBLANES, :] = new


def conv_ffn(x, g, state, wup_bf, wdw, bdw, wdn_bf, *, bb, tl, cw):
    b, l, d = x.shape
    f = wdn_bf.shape[0]
    kern = functools.partial(_ffn_kernel, bb=bb, tl=tl, f=f, cw=cw)
    return pl.pallas_call(
        kern,
        grid=(b // bb, l // tl),
        in_specs=[pl.BlockSpec((bb, tl, d), lambda i, j: (i, j, 0)),
                  _resident((1, d)),
                  pl.BlockSpec((bb, FFN_CONV_WIDTH - 1, 2 * f), lambda i, j: (i, 0, 0)),
                  _resident((d, 2 * f)),
                  _resident((FFN_CONV_WIDTH, 2 * f)),
                  _resident((1, 2 * f)),
                  _resident((f, d))],
        out_specs=[pl.BlockSpec((bb, tl, d), lambda i, j: (i, j, 0)),
                   pl.BlockSpec((bb, FFN_CONV_WIDTH - 1, 2 * f), lambda i, j: (i, 0, 0))],
        out_shape=[jax.ShapeDtypeStruct((b, l, d), F32),
                   jax.ShapeDtypeStruct((b, FFN_CONV_WIDTH - 1, 2 * f), F32)],
        scratch_shapes=[pltpu.VMEM((bb, SUBLANES + tl, 2 * f), F32),
                        pltpu.VMEM((bb * tl, f), BF16)],
        compiler_params=_params("parallel", "arbitrary"),
        name="conv_ffn",
    )(x, g.reshape(1, d), state, wup_bf, wdw, bdw.reshape(1, 2 * f), wdn_bf)


def _conv_mixer_kernel(x_ref, g_ref, st_ref, win_ref, wdw_ref, bdw_ref, lng_ref, lnb_ref, wout_ref,
                       y_ref, nst_ref, buf_sc, conv_sc, act_sc, *, bb, tl, bbk, rb, cbw):
    l = pl.program_id(1)
    m = bb * tl
    d = x_ref.shape[-1]
    c = wout_ref.shape[0]
    hist = CONV_WIDTH - 1
    head = 4 * SUBLANES
    off = head - hist

    @pl.when(l == 0)
    def _():
        buf_sc[:, off:head, :] = st_ref[...]

    x = x_ref[...].reshape(m, d)
    hn = _rms(x, g_ref[...]).astype(BF16)
    ag = _dot(hn, win_ref[...])
    u = ag[:, :c] * jax.nn.sigmoid(ag[:, c:])
    buf_sc[:, head:, :] = u.reshape(bb, tl, c)

    for b0 in range(0, bb, bbk):
        for r0 in range(0, tl, rb):
            lo = head + r0 - SUBLANES
            for c0 in range(0, c, cbw):
                cols = slice(c0, c0 + cbw)
                y = jnp.broadcast_to(bdw_ref[:, cols].reshape(1, 1, cbw), (bbk, rb, cbw))
                for s in range(SUBLANES):
                    z = None
                    for a in range(-(-CONV_WIDTH // SUBLANES)):
                        delay = SUBLANES * a + s
                        if delay >= CONV_WIDTH:
                            continue
                        w = jnp.concatenate([wdw_ref[hist - delay, :, cols]] * (rb // SUBLANES + 1), axis=0)[None]
                        term = w * buf_sc[b0:b0 + bbk, lo - SUBLANES * a:lo - SUBLANES * a + rb + SUBLANES, cols]
                        z = term if z is None else z + term
                    if s:
                        z = pltpu.roll(z, s, 1)
                    y = y + z[:, SUBLANES:, :]
                conv_sc[b0:b0 + bbk, r0:r0 + rb, cols] = y
            acc = conv_sc[b0:b0 + bbk, r0:r0 + rb, :]
            mu = jnp.mean(acc, axis=-1, keepdims=True)
            xc = acc - mu
            var = jnp.mean(xc * xc, axis=-1, keepdims=True)
            yn = xc * lax.rsqrt(var + EPS) * lng_ref[...].reshape(1, 1, c) + lnb_ref[...].reshape(1, 1, c)
            act_sc[b0:b0 + bbk, r0:r0 + rb, :] = (yn * jax.nn.sigmoid(yn)).astype(act_sc.dtype)
    y_ref[...] = (x + _dot(act_sc[...].reshape(m, c).astype(BF16), wout_ref[...])).reshape(bb, tl, d)

    new = buf_sc[:, tl + off:tl + head, :]

    @pl.when(l == pl.num_programs(1) - 1)
    def _():
        nst_ref[...] = new

    buf_sc[:, off:head, :] = new


def conv_mixer(x, g, state, win_bf, wdw, bdw, lng, lnb, wout_bf, *, bb, tl, bbk, rb):
    b, l, d = x.shape
    c = wout_bf.shape[0]
    hist = CONV_WIDTH - 1
    kern = functools.partial(_conv_mixer_kernel, bb=bb, tl=tl, bbk=bbk, rb=rb, cbw=128)
    return pl.pallas_call(
        kern,
        grid=(b // bb, l // tl),
        in_specs=[pl.BlockSpec((bb, tl, d), lambda i, j: (i, j, 0)),
                  _resident((1, d)),
                  pl.BlockSpec((bb, hist, c), lambda i, j: (i, 0, 0)),
                  _resident((d, 2 * c)),
                  _resident((CONV_WIDTH, SUBLANES, c)),
                  _resident((1, c)),
                  _resident((1, c)),
                  _resident((1, c)),
                  _resident((c, d))],
        out_specs=[pl.BlockSpec((bb, tl, d), lambda i, j: (i, j, 0)),
                   pl.BlockSpec((bb, hist, c), lambda i, j: (i, 0, 0))],
        out_shape=[jax.ShapeDtypeStruct((b, l, d), F32),
                   jax.ShapeDtypeStruct((b, hist, c), F32)],
        scratch_shapes=[pltpu.VMEM((bb, 4 * SUBLANES + tl, c), F32),
                        pltpu.VMEM((bb, tl, c), F32),
                        pltpu.VMEM((bb, tl, c), _mm_dtype(rb))],
        compiler_params=_params("parallel", "arbitrary"),
        name="conv_mixer",
    )(x, g.reshape(1, d), state, win_bf, jnp.broadcast_to(wdw[:, None, :], (CONV_WIDTH, SUBLANES, c)),
      bdw.reshape(1, c), lng.reshape(1, c), lnb.reshape(1, c), wout_bf)


def _gla_gate_kernel(g1_ref, w2_ref, b_ref, o_ref):
    z = _dot(g1_ref[...].astype(BF16), w2_ref[...]) + b_ref[...]
    o_ref[...] = _log_sigmoid(z) * (1.0 / GLA_GATE_NORM)


def gla_gate(h2, w2_pad_bf, b):
    m = h2.shape[0]
    dk = w2_pad_bf.shape[1]
    tm = min(m, 1024)
    g1_block = (2 * dk + 2 * GLA_HEADS * GLA_HEAD_V) // LANES
    return pl.pallas_call(
        _gla_gate_kernel,
        grid=(m // tm,),
        in_specs=[pl.BlockSpec((tm, LANES), lambda i: (i, g1_block)),
                  _resident((LANES, dk)),
                  _resident((1, dk))],
        out_specs=pl.BlockSpec((tm, dk), lambda i: (i, 0)),
        out_shape=jax.ShapeDtypeStruct((m, dk), F32),
        compiler_params=_params("parallel"),
        name="gla_gate",
    )(h2, w2_pad_bf, b.reshape(1, dk))


def _gla_rec_kernel(q_ref, k_ref, v_ref, g_ref, s0_ref, o_ref, s_ref, st_sc, *, tl, chunk):
    l = pl.program_id(1)
    mm = _mm_dtype(chunk)
    nh, hk, hv = GLA_HEADS, GLA_HEAD_K, GLA_HEAD_V

    @pl.when(l == 0)
    def _():
        for h in range(nh):
            st_sc[h] = s0_ref[0, h].T

    row = lax.broadcasted_iota(jnp.int32, (tl, tl), 0)
    col = lax.broadcasted_iota(jnp.int32, (tl, tl), 1)
    tril = ((row // chunk == col // chunk) & (col <= row)).astype(mm)
    gcum_all = _dot_exact_lhs01(tril, g_ref[0], mm)

    crow = lax.broadcasted_iota(jnp.int32, (chunk, chunk), 0)
    ccol = lax.broadcasted_iota(jnp.int32, (chunk, chunk), 1)
    causal = ccol <= crow

    for h in range(nh):
        kcols = slice(h * hk, (h + 1) * hk)
        vcols = slice(h * hv, (h + 1) * hv)
        q = q_ref[0, :, kcols] * (hk ** -0.5)
        k = k_ref[0, :, kcols]
        v = v_ref[0, :, vcols]
        gcum = gcum_all[:, kcols]
        st = st_sc[h]
        outs = []
        for c in range(tl // chunk):
            sl = slice(c * chunk, (c + 1) * chunk)
            gc = gcum[sl]
            gl = gc[chunk - 1:chunk]
            qe = (q[sl] * jnp.exp(gc)).astype(mm)
            ke = (k[sl] * jnp.exp(-gc)).astype(mm)
            kd = (k[sl] * jnp.exp(gl - gc)).astype(mm)
            vc = v[sl].astype(mm)
            scores = jnp.where(causal, _dot_nt(qe, ke), 0.0)
            outs.append(_dot_nt(qe, st.astype(mm)) + _dot(scores.astype(mm), vc))
            st = st * jnp.exp(gl) + _dot_tn(vc, kd)
        o_ref[0, :, vcols] = outs[0] if len(outs) == 1 else jnp.concatenate(outs, axis=0)
        st_sc[h] = st

    @pl.when(l == pl.num_programs(1) - 1)
    def _():
        for h in range(nh):
            s_ref[0, h] = st_sc[h].T


def gla_recurrence(h3, gk3, s0, *, tl, chunk):
    b, l, _ = h3.shape
    hk, hv, nh = GLA_HEAD_K, GLA_HEAD_V, GLA_HEADS
    dk, dv = nh * hk, nh * hv
    kern = functools.partial(_gla_rec_kernel, tl=tl, chunk=chunk)
    return pl.pallas_call(
        kern,
        grid=(b, l // tl),
        in_specs=[pl.BlockSpec((1, tl, dk), lambda i, j: (i, j, 0)),
                  pl.BlockSpec((1, tl, dk), lambda i, j: (i, j, 1)),
                  pl.BlockSpec((1, tl, dv), lambda i, j: (i, j, (2 * dk) // dv)),
                  pl.BlockSpec((1, tl, dk), lambda i, j: (i, j, 0)),
                  pl.BlockSpec((1, nh, hk, hv), lambda i, j: (i, 0, 0, 0))],
        out_specs=[pl.BlockSpec((1, tl, dv), lambda i, j: (i, j, 0)),
                   pl.BlockSpec((1, nh, hk, hv), lambda i, j: (i, 0, 0, 0))],
        out_shape=[jax.ShapeDtypeStruct((b, l, dv), F32),
                   jax.ShapeDtypeStruct((b, nh, hk, hv), F32)],
        scratch_shapes=[pltpu.VMEM((nh, hv, hk), F32)],
        compiler_params=_params("parallel", "arbitrary"),
        name="gla_recurrence",
    )(h3, h3, h3, gk3, s0)


def _gla_out_kernel(x_ref, o_ref, r_ref, gn_ref, w_ref, y_ref):
    o = o_ref[...]
    gn = gn_ref[...]
    parts = []
    for h in range(GLA_HEADS):
        parts.append(_rms(o[:, h * GLA_HEAD_V:(h + 1) * GLA_HEAD_V], gn))
    on = jnp.concatenate(parts, axis=-1)
    r = r_ref[...]
    gated = (on * (r * jax.nn.sigmoid(r))).astype(BF16)
    y_ref[...] = x_ref[...] + _dot(gated, w_ref[...])


def gla_out(x2, o2, h2, gn, wout_bf):
    m, d = x2.shape
    dv = o2.shape[1]
    tm = min(m, 512)
    r_block = (2 * GLA_HEADS * GLA_HEAD_K + dv) // dv
    return pl.pallas_call(
        _gla_out_kernel,
        grid=(m // tm,),
        in_specs=[pl.BlockSpec((tm, d), lambda i: (i, 0)),
                  pl.BlockSpec((tm, dv), lambda i: (i, 0)),
                  pl.BlockSpec((tm, dv), lambda i: (i, r_block)),
                  _resident((1, GLA_HEAD_V)),
                  _resident((dv, d))],
        out_specs=pl.BlockSpec((tm, d), lambda i: (i, 0)),
        out_shape=jax.ShapeDtypeStruct((m, d), F32),
        compiler_params=_params("parallel"),
        name="gla_out",
    )(x2, o2, h2, gn.reshape(1, GLA_HEAD_V), wout_bf)


def _head_norm(x, g2):
    lo = lax.broadcasted_iota(jnp.int32, (1, LANES), 1) < FOX_HEAD_DIM
    parts = []
    for j in range(x.shape[-1] // LANES):
        xb = x[:, j * LANES:(j + 1) * LANES]
        sq = xb * xb
        s_lo = jnp.sum(jnp.where(lo, sq, 0.0), axis=-1, keepdims=True)
        s_hi = jnp.sum(jnp.where(lo, 0.0, sq), axis=-1, keepdims=True)
        ms = jnp.where(lo, s_lo, s_hi) * (1.0 / FOX_HEAD_DIM)
        parts.append(xb * lax.rsqrt(ms + EPS) * g2)
    return jnp.concatenate(parts, axis=-1)


def _aug_select():
    sq = np.zeros((4 * LANES, FOX_HEADS * LANES), np.float32)
    sk = np.zeros((4 * LANES, FOX_HEADS * LANES), np.float32)
    for h in range(FOX_HEADS):
        a0 = h * LANES + (FOX_HEAD_DIM if h % 2 == 0 else 0)
        for i in range(3):
            sq[i * LANES + h, a0 + i] = 1.0
            sq[3 * LANES, a0 + 3 + i] = 1.0
            sk[3 * LANES, a0 + i] = 1.0
            sk[i * LANES + h, a0 + 3 + i] = -1.0
    return sq, sk


def _fox_prep_prompt_kernel(x_ref, g_ref, w_ref, qg_ref, kg_ref, bf_ref, sq_ref, sk_ref,
                            qa_ref, kn_ref, ka_ref, vo_ref, va_ref, lf_ref, og_ref, carry_sc, *, tl):
    l = pl.program_id(1)

    @pl.when(l == 0)
    def _():
        carry_sc[...] = jnp.zeros_like(carry_sc)

    d = FOX_HEADS * FOX_HEAD_DIM
    hp = _dot(_rms(x_ref[0], g_ref[...]).astype(BF16), w_ref[...])
    qn = _head_norm(hp[:, :d], qg_ref[...]) * (FOX_HEAD_DIM ** -0.5 * LOG2E)
    kn = _head_norm(hp[:, d:2 * d], kg_ref[...])
    v = hp[:, 2 * d:3 * d]
    kn_ref[0] = kn
    vo_ref[0] = v
    og_ref[0] = hp[:, 3 * d:4 * d]

    lf = _log_sigmoid(hp[:, 4 * d:] + bf_ref[...])
    lf_ref[0] = lf[:, :FOX_HEADS]
    row = lax.broadcasted_iota(jnp.int32, (tl, tl), 0)
    col = lax.broadcasted_iota(jnp.int32, (tl, tl), 1)
    c = _dot_exact_lhs01((col <= row).astype(BF16), lf, BF16) + carry_sc[...]
    carry_sc[...] = c[tl - 1:tl]
    pieces = jnp.concatenate(list(_split3(c * LOG2E)) + [jnp.ones((tl, LANES), F32)], axis=1).astype(BF16)
    qaug = _dot(pieces, sq_ref[...])
    kaug = _dot(pieces, sk_ref[...])

    lane = lax.broadcasted_iota(jnp.int32, (1, LANES), 1)
    lo = lane < FOX_HEAD_DIM
    for h in range(FOX_HEADS):
        src = slice((h // 2) * LANES, (h // 2 + 1) * LANES)
        dst = slice(h * LANES, (h + 1) * LANES)
        own = lo if h % 2 == 0 else jnp.logical_not(lo)
        vaug = jnp.where(lane == (FOX_HEAD_DIM if h % 2 == 0 else 0), 1.0, 0.0)
        qa_ref[0, :, dst] = jnp.where(own, qn[:, src], qaug[:, dst]).astype(BF16)
        ka_ref[0, :, dst] = jnp.where(own, kn[:, src], kaug[:, dst]).astype(BF16)
        va_ref[0, :, dst] = jnp.where(own, v[:, src], vaug).astype(BF16)


def fox_prep_prompt(x3, g, win_bf, qg, kg, bf, *, tl):
    b, l, dm = x3.shape
    d = FOX_HEADS * FOX_HEAD_DIM
    da = FOX_HEADS * LANES
    qg2 = jnp.tile(qg, 2).reshape(1, LANES)
    kg2 = jnp.tile(kg, 2).reshape(1, LANES)
    bf_pad = jnp.pad(bf, (0, LANES - FOX_HEADS)).reshape(1, LANES)
    sq, sk = _aug_select()
    row3 = lambda i, j: (i, j, 0)
    kern = functools.partial(_fox_prep_prompt_kernel, tl=tl)
    return pl.pallas_call(
        kern,
        grid=(b, l // tl),
        in_specs=[pl.BlockSpec((1, tl, dm), row3),
                  _resident((1, dm)),
                  _resident(win_bf.shape),
                  _resident((1, LANES)), _resident((1, LANES)), _resident((1, LANES)),
                  _resident(sq.shape), _resident(sk.shape)],
        out_specs=[pl.BlockSpec((1, tl, da), row3),
                   pl.BlockSpec((1, tl, d), row3),
                   pl.BlockSpec((1, tl, da), row3),
                   pl.BlockSpec((1, tl, d), row3),
                   pl.BlockSpec((1, tl, da), row3),
                   pl.BlockSpec((1, tl, FOX_HEADS), row3),
                   pl.BlockSpec((1, tl, d), row3)],
        out_shape=[jax.ShapeDtypeStruct((b, l, da), BF16),
                   jax.ShapeDtypeStruct((b, l, d), F32),
                   jax.ShapeDtypeStruct((b, l, da), BF16),
                   jax.ShapeDtypeStruct((b, l, d), F32),
                   jax.ShapeDtypeStruct((b, l, da), BF16),
                   jax.ShapeDtypeStruct((b, l, FOX_HEADS), F32),
                   jax.ShapeDtypeStruct((b, l, d), F32)],
        scratch_shapes=[pltpu.VMEM((1, LANES), F32)],
        compiler_params=_params("parallel", "arbitrary"),
        name="fox_prep_prompt",
    )(x3, g.reshape(1, dm), win_bf, qg2, kg2, bf_pad, jnp.asarray(sq, BF16), jnp.asarray(sk, BF16))


def _fox_prep_sample_kernel(q_ref, k_ref, v_ref, fl_ref, qg_ref, kg_ref, bf_ref, qo_ref, kn_ref, vo_ref, lf_ref):
    qo_ref[...] = _head_norm(q_ref[...], qg_ref[...]) * (FOX_HEAD_DIM ** -0.5)
    kn_ref[...] = _head_norm(k_ref[...], kg_ref[...])
    vo_ref[...] = v_ref[...]
    lf_ref[...] = _log_sigmoid(fl_ref[...] + bf_ref[...])[:, :FOX_HEADS]


def fox_prep_sample(h2, qg, kg, bf):
    m = h2.shape[0]
    d = FOX_HEADS * FOX_HEAD_DIM
    fl_block = (4 * d) // LANES
    qg2 = jnp.tile(qg, 2).reshape(1, LANES)
    kg2 = jnp.tile(kg, 2).reshape(1, LANES)
    bf_pad = jnp.pad(bf, (0, LANES - FOX_HEADS)).reshape(1, LANES)
    full = lambda i: (0, 0)
    return pl.pallas_call(
        _fox_prep_sample_kernel,
        grid=(1,),
        in_specs=[pl.BlockSpec((m, d), lambda i: (0, 0)),
                  pl.BlockSpec((m, d), lambda i: (0, 1)),
                  pl.BlockSpec((m, d), lambda i: (0, 2)),
                  pl.BlockSpec((m, LANES), lambda i: (0, fl_block)),
                  _resident((1, LANES)), _resident((1, LANES)), _resident((1, LANES))],
        out_specs=[pl.BlockSpec((m, d), full), pl.BlockSpec((m, d), full), pl.BlockSpec((m, d), full),
                   pl.BlockSpec((m, FOX_HEADS), full)],
        out_shape=[jax.ShapeDtypeStruct((m, d), F32), jax.ShapeDtypeStruct((m, d), F32),
                   jax.ShapeDtypeStruct((m, d), F32), jax.ShapeDtypeStruct((m, FOX_HEADS), F32)],
        compiler_params=_params("arbitrary"),
        name="fox_prep_sample",
    )(h2, h2, h2, h2, qg2, kg2, bf_pad)


def _flash_kernel(qt_ref, kt_ref, q_ref, k_ref, v_ref, o_ref, m_sc, acc_sc, *, t):
    step = pl.program_id(1)
    qi = qt_ref[step]
    ki = kt_ref[step]
    lo = lax.broadcasted_iota(jnp.int32, (1, LANES), 1) < FOX_HEAD_DIM

    @pl.when(ki == 0)
    def _():
        m_sc[...] = jnp.full_like(m_sc, -jnp.inf)
        acc_sc[...] = jnp.zeros_like(acc_sc)

    def head_update(h, diagonal):
        cols = pl.ds(pl.multiple_of(h * LANES, LANES), LANES)
        s = _dot_nt(q_ref[0, :, cols], k_ref[0, :, cols])
        if diagonal:
            row = lax.broadcasted_iota(jnp.int32, (t, t), 0)
            col = lax.broadcasted_iota(jnp.int32, (t, t), 1)
            s = jnp.where(col <= row, s, -jnp.inf)
        m_prev = m_sc[h]
        m_new = jnp.maximum(m_prev, jnp.max(s, axis=-1, keepdims=True))
        alpha = jnp.exp2(m_prev - m_new)
        p = jnp.exp2(s - jnp.concatenate([m_new] * (t // LANES), axis=1))
        acc_sc[h] = acc_sc[h] * alpha + _dot(p.astype(BF16), v_ref[0, :, cols])
        m_sc[h] = m_new

    @pl.when(ki < qi)
    def _():
        def body(j, carry):
            head_update(2 * j, False)
            head_update(2 * j + 1, False)
            return carry

        lax.fori_loop(0, FOX_HEADS // 2, body, 0, unroll=8)

    @pl.when(ki == qi)
    def _():
        def body(j, carry):
            head_update(2 * j, True)
            head_update(2 * j + 1, True)
            a0 = acc_sc[2 * j]
            a1 = acc_sc[2 * j + 1]
            out = jnp.where(lo, a0 / a0[:, FOX_HEAD_DIM:FOX_HEAD_DIM + 1], a1 / a1[:, 0:1])
            o_ref[0, :, pl.ds(pl.multiple_of(j * LANES, LANES), LANES)] = out
            return carry

        lax.fori_loop(0, FOX_HEADS // 2, body, 0)


def fox_flash(qa, ka, va, *, t):
    b, l, da = qa.shape
    d = FOX_HEADS * FOX_HEAD_DIM
    n = l // t
    qi = np.concatenate([np.full(i + 1, i, np.int32) for i in range(n)])
    ki = np.concatenate([np.arange(i + 1, dtype=np.int32) for i in range(n)])
    kern = functools.partial(_flash_kernel, t=t)
    grid_spec = pltpu.PrefetchScalarGridSpec(
        num_scalar_prefetch=2,
        grid=(b, len(qi)),
        in_specs=[pl.BlockSpec((1, t, da), lambda i, s, qt, kt: (i, qt[s], 0)),
                  pl.BlockSpec((1, t, da), lambda i, s, qt, kt: (i, kt[s], 0)),
                  pl.BlockSpec((1, t, da), lambda i, s, qt, kt: (i, kt[s], 0))],
        out_specs=pl.BlockSpec((1, t, d), lambda i, s, qt, kt: (i, qt[s], 0)),
        scratch_shapes=[pltpu.VMEM((FOX_HEADS, t, LANES), F32),
                        pltpu.VMEM((FOX_HEADS, t, LANES), F32)])
    return pl.pallas_call(
        kern,
        grid_spec=grid_spec,
        out_shape=jax.ShapeDtypeStruct((b, l, d), F32),
        compiler_params=_params("parallel", "arbitrary"),
        name="fox_flash",
    )(jnp.asarray(qi), jnp.asarray(ki), qa, ka, va)


def _decode_kernel(pt_ref, q_ref, kn_ref, vn_ref, lfn_ref, *refs, lq, page, pg):
    del pt_ref
    kt_refs, vt_refs, lft_refs = refs[:pg], refs[pg:2 * pg], refs[2 * pg:3 * pg]
    o_ref = refs[3 * pg]
    qbd_sc, m_sc, l_sc, acc_sc, cq_sc, carry_sc = refs[3 * pg + 1:]
    p = pl.program_id(1)
    nh, hd = FOX_HEADS, FOX_HEAD_DIM
    rows = nh * lq
    d = nh * hd

    uj = lax.broadcasted_iota(jnp.int32, (page, 2 * page), 0)
    us = lax.broadcasted_iota(jnp.int32, (page, 2 * page), 1)
    u2 = ((uj > us) | (us >= page)).astype(BF16)

    def later_sums(x, keep):
        out = jnp.zeros((x.shape[0], 2 * page), F32)
        for piece in _split3(x):
            if keep is not None:
                piece = jnp.where(keep, piece, 0.0)
            out = out + _dot(piece.astype(BF16), u2)
        return out

    def head_rows(x):
        return jnp.concatenate([jnp.broadcast_to(x[h:h + 1], (lq, x.shape[1])) for h in range(nh)], axis=0)

    @pl.when(p == 0)
    def _():
        qt = jnp.concatenate([q_ref[0]] * nh, axis=0)
        rr = lax.broadcasted_iota(jnp.int32, (rows, d), 0)
        cc = lax.broadcasted_iota(jnp.int32, (rows, d), 1)
        qbd_sc[...] = jnp.where(rr // lq == cc // hd, qt, 0.0)

        pad = page - lq
        kn = jnp.concatenate([kn_ref[0], jnp.zeros((pad, d), F32)], axis=0)
        vn = jnp.concatenate([vn_ref[0], jnp.zeros((pad, d), F32)], axis=0)
        lfn = jnp.concatenate([lfn_ref[0], jnp.zeros((pad, nh), F32)], axis=0)
        er = lax.broadcasted_iota(jnp.int32, (rows, nh), 0)
        ec = lax.broadcasted_iota(jnp.int32, (rows, nh), 1)
        expand = (er // lq == ec).astype(BF16)
        t = lax.broadcasted_iota(jnp.int32, (rows, page), 0) % lq
        j = lax.broadcasted_iota(jnp.int32, (rows, page), 1)
        sums = jnp.zeros((rows, 2 * page), F32)
        for piece in _split3(lfn):
            e = jnp.where(j <= t, _dot_nt(expand, piece.astype(BF16)), 0.0)
            sums = sums + _dot(e.astype(BF16), u2)
        cq_sc[...] = sums[:, page:]
        s = jnp.where(j <= t, _dot_nt(qbd_sc[...], kn) + sums[:, :page], -jnp.inf)
        m_cur = jnp.max(s, axis=-1, keepdims=True)
        pexp = jnp.exp(s - m_cur)
        m_sc[...] = jnp.broadcast_to(m_cur, (rows, LANES))
        l_sc[...] = jnp.broadcast_to(jnp.sum(pexp, axis=-1, keepdims=True), (rows, LANES))
        acc_sc[...] = _dot(pexp, vn)
        carry_sc[...] = jnp.zeros_like(carry_sc)

    lft = jnp.concatenate([r[0] for r in lft_refs], axis=0)
    sums = later_sums(lft, None)
    carry = carry_sc[...]
    tiles = []
    for i in range(pg):
        blk = slice(i * nh, (i + 1) * nh)
        bias = head_rows(sums[blk, :page] + carry) + cq_sc[...]
        carry = carry + sums[blk, page:]
        tiles.append(_dot(qbd_sc[...], kt_refs[i][0]) + bias)
    carry_sc[...] = carry
    s = jnp.concatenate(tiles, axis=1)
    m_prev = m_sc[...]
    m_new = jnp.maximum(m_prev, jnp.max(s, axis=-1, keepdims=True))
    alpha = jnp.exp(m_prev - m_new)
    pexp = jnp.exp(s - jnp.concatenate([m_new] * pg, axis=1))
    l_sc[...] = alpha * l_sc[...] + jnp.sum(pexp, axis=-1, keepdims=True)
    m_sc[...] = m_new
    pv = _dot_nt(pexp[:, :page], vt_refs[0][0])
    for i in range(1, pg):
        pv = pv + _dot_nt(pexp[:, i * page:(i + 1) * page], vt_refs[i][0])
    for c in range(d // LANES):
        cols = slice(c * LANES, (c + 1) * LANES)
        acc_sc[:, cols] = acc_sc[:, cols] * alpha + pv[:, cols]

    @pl.when(p == pl.num_programs(1) - 1)
    def _():
        rr = lax.broadcasted_iota(jnp.int32, (rows, d), 0)
        cc = lax.broadcasted_iota(jnp.int32, (rows, d), 1)
        inv = 1.0 / l_sc[...]
        parts = [acc_sc[:, c * LANES:(c + 1) * LANES] * inv for c in range(d // LANES)]
        on = jnp.where(rr // lq == cc // hd, jnp.concatenate(parts, axis=-1), 0.0)
        o_ref[0] = jnp.sum(on.reshape(nh, lq, d), axis=0)


def fox_decode(q3, kn3, vn3, lfn3, kpool_t, vpool_t, lfpool_t, page_table, *, pg):
    b, lq, d = q3.shape
    nh = FOX_HEADS
    n_pages = page_table.shape[1]
    page = lfpool_t.shape[2]
    rows = nh * lq
    assert page == LANES and rows == LANES and n_pages % pg == 0

    def new_map(i, p, pt):
        return (i, 0, 0)

    def page_map(k):
        return lambda i, p, pt: (pt[i, n_pages - 1 - (p * pg + k)], 0, 0)

    kern = functools.partial(_decode_kernel, lq=lq, page=page, pg=pg)
    grid_spec = pltpu.PrefetchScalarGridSpec(
        num_scalar_prefetch=1,
        grid=(b, n_pages // pg),
        in_specs=[pl.BlockSpec((1, lq, d), new_map),
                  pl.BlockSpec((1, lq, d), new_map),
                  pl.BlockSpec((1, lq, d), new_map),
                  pl.BlockSpec((1, lq, nh), new_map)]
                 + [pl.BlockSpec((1, d, page), page_map(k)) for k in range(pg)]
                 + [pl.BlockSpec((1, d, page), page_map(k)) for k in range(pg)]
                 + [pl.BlockSpec((1, nh, page), page_map(k)) for k in range(pg)],
        out_specs=pl.BlockSpec((1, lq, d), new_map),
        scratch_shapes=[pltpu.VMEM((rows, d), F32),
                        pltpu.VMEM((rows, LANES), F32),
                        pltpu.VMEM((rows, LANES), F32),
                        pltpu.VMEM((rows, d), F32),
                        pltpu.VMEM((rows, LANES), F32),
                        pltpu.VMEM((nh, LANES), F32)])
    return pl.pallas_call(
        kern,
        grid_spec=grid_spec,
        out_shape=jax.ShapeDtypeStruct((b, lq, d), F32),
        compiler_params=_params("parallel", "arbitrary"),
        name="fox_decode",
    )(page_table, q3, kn3, vn3, lfn3, *([kpool_t] * pg), *([vpool_t] * pg), *([lfpool_t] * pg))


def _fox_out_kernel(x_ref, o_ref, og_ref, w_ref, y_ref):
    gated = (o_ref[...] * jax.nn.sigmoid(og_ref[...])).astype(BF16)
    y_ref[...] = x_ref[...] + _dot(gated, w_ref[...])


def fox_out(x2, o2, gate2, gate_block, wout_bf):
    m, d = x2.shape
    tm = min(m, 512)
    return pl.pallas_call(
        _fox_out_kernel,
        grid=(m // tm,),
        in_specs=[pl.BlockSpec((tm, d), lambda i: (i, 0)),
                  pl.BlockSpec((tm, d), lambda i: (i, 0)),
                  pl.BlockSpec((tm, d), lambda i: (i, gate_block)),
                  _resident((d, d))],
        out_specs=pl.BlockSpec((tm, d), lambda i: (i, 0)),
        out_shape=jax.ShapeDtypeStruct((m, d), F32),
        compiler_params=_params("parallel"),
        name="fox_out",
    )(x2, o2, gate2, wout_bf)


def _tiles(x):
    b, l, _ = x.shape
    if l >= 512:
        return 1, 512
    return b, l


def _conv_layer(x, g, state, prm):
    bb, tl = _tiles(x)
    tl = min(tl, 256)
    bbk, rb = (1, 64) if tl >= 64 else (64 // tl, tl)
    return conv_mixer(x, g, state, *prm, bb=bb, tl=tl, bbk=bbk, rb=rb)


def _ffn_layer(x, g, state, prm):
    bb, tl = _tiles(x)
    return conv_ffn(x, g, state, *prm, bb=bb, tl=tl, cw=FFN_CHUNK)


def _gla_layer(x, g, s0, prm):
    win_bf, w2_bf, b_gk, gn, wout_bf = prm
    b, l, d = x.shape
    x2 = x.reshape(b * l, d)
    h2 = norm_matmul(x2, g, win_bf)
    gk2 = gla_gate(h2, w2_bf, b_gk)
    chunk = GLA_CHUNK if l % GLA_CHUNK == 0 else l
    tl = 4 * chunk if l % (4 * chunk) == 0 else chunk
    o3, s = gla_recurrence(h2.reshape(b, l, -1), gk2.reshape(b, l, -1), s0, tl=tl, chunk=chunk)
    y2 = gla_out(x2, o3.reshape(b * l, -1), h2, gn, wout_bf)
    return y2.reshape(b, l, d), s


def _fox_layer(x, g, cache, prm):
    win_bf, b_f, qg, kg, wout_bf = prm
    b, l, d = x.shape
    x2 = x.reshape(b * l, d)
    if cache is None:
        qa, kn, ka, vo, va, lf, og = fox_prep_prompt(x, g, win_bf, qg, kg, b_f, tl=512)
        o3 = fox_flash(qa, ka, va, t=512)
        y2 = fox_out(x2, o3.reshape(b * l, d), og.reshape(b * l, d), 0, wout_bf)
    else:
        kpool_t, vpool_t, lfpool_t, page_table = cache
        h2 = norm_matmul(x2, g, win_bf)
        q2, kn, vo, lf = fox_prep_sample(h2, qg, kg, b_f)
        o3 = fox_decode(q2.reshape(b, l, d), kn.reshape(b, l, d), vo.reshape(b, l, d), lf.reshape(b, l, FOX_HEADS),
                        kpool_t, vpool_t, lfpool_t, page_table, pg=8)
        y2 = fox_out(x2, o3.reshape(b * l, d), h2, 3, wout_bf)
    shp = (b, l, FOX_HEADS, FOX_HEAD_DIM)
    return y2.reshape(b, l, d), kn.reshape(shp), vo.reshape(shp), lf.reshape(b, l, FOX_HEADS)


def _pad_cols(w, n):
    return jnp.pad(w, ((0, 0), (0, n - w.shape[1])))


def kernel(x_prompt, x_sample, state_conv, state_gla, cache_fox_k, cache_fox_v, cache_fox_logf, state_ffn_conv, page_table,
           norm_mix_g, norm_ffn_g, w_conv_in, w_conv_dw, b_conv_dw, conv_ln_g, conv_ln_b, w_conv_out,
           w_gla_in, w_gla_gk2, b_gla_gk, gla_norm_g, w_gla_out,
           w_fox_in, b_fox_f, fox_qn_g, fox_kn_g, w_fox_out,
           w_ffn_up, w_ffn_dw, b_ffn_dw, w_ffn_down):
    depth = norm_mix_g.shape[0]
    bp = x_prompt.shape[0]
    d = x_prompt.shape[-1]
    xp, xs = x_prompt, x_sample
    conv_p, conv_s, gla_p, gla_s = [], [], [], []
    fk_p, fk_s, fv_p, fv_s, fl_p, fl_s = [], [], [], [], [], []
    ffn_p, ffn_s = [], []
    for i in range(depth):
        m, j = i % 3, i // 3
        if m == 0:
            prm = (w_conv_in[j].astype(BF16), w_conv_dw[j], b_conv_dw[j], conv_ln_g[j], conv_ln_b[j],
                   w_conv_out[j].astype(BF16))
            zero = jnp.zeros((bp,) + state_conv.shape[2:], F32)
            xp, st_p = _conv_layer(xp, norm_mix_g[i], zero, prm)
            xs, st_s = _conv_layer(xs, norm_mix_g[i], state_conv[j], prm)
            conv_p.append(st_p)
            conv_s.append(st_s)
        elif m == 1:
            n_in = w_gla_in.shape[-1]
            n_pad = -(-n_in // (5 * LANES)) * (5 * LANES)
            w2 = jnp.pad(w_gla_gk2[j], ((0, LANES - GLA_GATE_RANK), (0, 0))).astype(BF16)
            prm = (_pad_cols(w_gla_in[j], n_pad).astype(BF16), w2, b_gla_gk[j], gla_norm_g[j],
                   w_gla_out[j].astype(BF16))
            zero = jnp.zeros((bp,) + state_gla.shape[2:], F32)
            xp, st_p = _gla_layer(xp, norm_mix_g[i], zero, prm)
            xs, st_s = _gla_layer(xs, norm_mix_g[i], state_gla[j], prm)
            gla_p.append(st_p)
            gla_s.append(st_s)
        else:
            n_pad = 4 * d + LANES
            prm = (_pad_cols(w_fox_in[j], n_pad).astype(BF16), b_fox_f[j], fox_qn_g[j], fox_kn_g[j],
                   w_fox_out[j].astype(BF16))
            n_pool, page = cache_fox_k.shape[1], cache_fox_k.shape[2]
            cache = (jnp.transpose(cache_fox_k[j], (0, 2, 3, 1)).reshape(n_pool, d, page),
                     jnp.transpose(cache_fox_v[j], (0, 2, 3, 1)).reshape(n_pool, d, page),
                     jnp.transpose(cache_fox_logf[j], (0, 2, 1)), page_table)
            xp, kp, vp, lp = _fox_layer(xp, norm_mix_g[i], None, prm)
            xs, kn, vn, ln = _fox_layer(xs, norm_mix_g[i], cache, prm)
            fk_p.append(kp)
            fk_s.append(kn)
            fv_p.append(vp)
            fv_s.append(vn)
            fl_p.append(lp)
            fl_s.append(ln)
        fprm = (w_ffn_up[i].astype(BF16), w_ffn_dw[i], b_ffn_dw[i], w_ffn_down[i].astype(BF16))
        zero = jnp.zeros((bp,) + state_ffn_conv.shape[2:], F32)
        xp, st_p = _ffn_layer(xp, norm_ffn_g[i], zero, fprm)
        xs, st_s = _ffn_layer(xs, norm_ffn_g[i], state_ffn_conv[i], fprm)
        ffn_p.append(st_p)
        ffn_s.append(st_s)
    return (xp, xs, jnp.stack(conv_p), jnp.stack(conv_s), jnp.stack(gla_p), jnp.stack(gla_s),
            jnp.stack(fk_p), jnp.stack(fk_s), jnp.stack(fv_p), jnp.stack(fv_s), jnp.stack(fl_p), jnp.stack(fl_s),
            jnp.stack(ffn_p), jnp.stack(ffn_s))
```

```python
import functools

import jax
import jax.numpy as jnp
import numpy as np
from jax import lax
from jax.experimental import pallas as pl
from jax.experimental.pallas import tpu as pltpu

F32 = jnp.float32
BF16 = jnp.bfloat16
EPS = 1e-6
LOG2E = 1.4426950408889634

LANES = 128
SUBLANES = 8
VMEM_LIMIT_BYTES = 58 * 1024 * 1024

GLA_HEADS = 4
GLA_HEAD_K = 128
GLA_HEAD_V = 256
GLA_GATE_RANK = 16
GLA_GATE_NORM = 16.0
GLA_CHUNK = 64
FOX_HEADS = 16
FOX_HEAD_DIM = 64
CONV_WIDTH = 31
FFN_CONV_WIDTH = 3
FFN_CHUNK = 512


def _params(*sem):
    return pltpu.CompilerParams(dimension_semantics=sem, vmem_limit_bytes=VMEM_LIMIT_BYTES)


def _resident(shape):
    nd = len(shape)
    return pl.BlockSpec(shape, lambda *_: (0,) * nd, pipeline_mode=pl.Buffered(1))


def _dot(a, b):
    return jnp.dot(a, b, preferred_element_type=F32)


def _dot_nt(a, b):
    return lax.dot_general(a, b, (((1,), (1,)), ((), ())), preferred_element_type=F32)


def _dot_tn(a, b):
    return lax.dot_general(a, b, (((0,), (0,)), ((), ())), preferred_element_type=F32)


def _split3(x):
    hi = x.astype(BF16).astype(F32)
    r = x - hi
    mid = r.astype(BF16).astype(F32)
    lo = (r - mid).astype(BF16).astype(F32)
    return hi, mid, lo


def _dot_exact_lhs01(t, x, mm):
    hi, mid, lo = _split3(x)
    return _dot(t, hi.astype(mm)) + _dot(t, mid.astype(mm)) + _dot(t, lo.astype(mm))


def _rms(x, g):
    return x * lax.rsqrt(jnp.mean(x * x, axis=-1, keepdims=True) + EPS) * g


def _log_sigmoid(z):
    return jnp.minimum(z, 0.0) - jnp.log1p(jnp.exp(-jnp.abs(z)))


def _mm_dtype(rows):
    return BF16 if rows % 16 == 0 else F32


def _norm_matmul_kernel(x_ref, g_ref, w_ref, o_ref):
    o_ref[...] = _dot(_rms(x_ref[...], g_ref[...]).astype(BF16), w_ref[...])


def norm_matmul(x2, g, w_bf):
    m, d = x2.shape
    n = w_bf.shape[1]
    tm = min(m, 512)
    return pl.pallas_call(
        _norm_matmul_kernel,
        grid=(m // tm,),
        in_specs=[pl.BlockSpec((tm, d), lambda i: (i, 0)),
                  _resident((1, d)),
                  _resident((d, n))],
        out_specs=pl.BlockSpec((tm, n), lambda i: (i, 0)),
        out_shape=jax.ShapeDtypeStruct((m, n), F32),
        compiler_params=_params("parallel"),
        name="norm_matmul",
    )(x2, g.reshape(1, d), w_bf)


def _ffn_kernel(x_ref, g_ref, st_ref, wup_ref, wdw_ref, bdw_ref, wdn_ref, y_ref, nst_ref,
                u_sc, act_sc, *, bb, tl, f, cw):
    l = pl.program_id(1)
    m = bb * tl
    d = x_ref.shape[-1]
    pre = SUBLANES - (FFN_CONV_WIDTH - 1)

    @pl.when(l == 0)
    def _():
        u_sc[:, pre:SUBLANES, :] = st_ref[...]

    x = x_ref[...].reshape(m, d)
    hn = _rms(x, g_ref[...]).astype(BF16)
    for c0 in range(0, f, cw):
        cn = min(cw, f - c0)
        halves = []
        for col in (c0, f + c0):
            cols = slice(col, col + cn)
            u_sc[:, SUBLANES:, cols] = _dot(hn, wup_ref[:, cols]).reshape(bb, tl, cn)
            acc = bdw_ref[:, cols]
            for w in range(FFN_CONV_WIDTH):
                acc = acc + wdw_ref[w:w + 1, cols] * u_sc[:, pre + w:pre + w + tl, cols]
            halves.append(acc)
        ya, yb = halves
        act_sc[:, c0:c0 + cn] = (ya * jax.nn.sigmoid(ya) * yb).reshape(m, cn).astype(BF16)
    y_ref[...] = (x + _dot(act_sc[...], wdn_ref[...])).reshape(bb, tl, d)

    new = u_sc[:, tl + pre:tl + SUBLANES, :]

    @pl.when(l == pl.num_programs(1) - 1)
    def _():
        nst_ref[...] = new

    u_sc[:, pre:SUBLANES, :] = new


def conv_ffn(x, g, state, wup_bf, wdw, bdw, wdn_bf, *, bb, tl, cw):
    b, l, d = x.shape
    f = wdn_bf.shape[0]
    kern = functools.partial(_ffn_kernel, bb=bb, tl=tl, f=f, cw=cw)
    return pl.pallas_call(
        kern,
        grid=(b // bb, l // tl),
        in_specs=[pl.BlockSpec((bb, tl, d), lambda i, j: (i, j, 0)),
                  _resident((1, d)),
                  pl.BlockSpec((bb, FFN_CONV_WIDTH - 1, 2 * f), lambda i, j: (i, 0, 0)),
                  _resident((d, 2 * f)),
                  _resident((FFN_CONV_WIDTH, 2 * f)),
                  _resident((1, 2 * f)),
                  _resident((f, d))],
        out_specs=[pl.BlockSpec((bb, tl, d), lambda i, j: (i, j, 0)),
                   pl.BlockSpec((bb, FFN_CONV_WIDTH - 1, 2 * f), lambda i, j: (i, 0, 0))],
        out_shape=[jax.ShapeDtypeStruct((b, l, d), F32),
                   jax.ShapeDtypeStruct((b, FFN_CONV_WIDTH - 1, 2 * f), F32)],
        scratch_shapes=[pltpu.VMEM((bb, SUBLANES + tl, 2 * f), F32),
                        pltpu.VMEM((bb * tl, f), BF16)],
        compiler_params=_params("parallel", "arbitrary"),
        name="conv_ffn",
    )(x, g.reshape(1, d), state, wup_bf, wdw, bdw.reshape(1, 2 * f), wdn_bf)


def _conv_mixer_kernel(x_ref, g_ref, st_ref, win_ref, wdw_ref, bdw_ref, lng_ref, lnb_ref, wout_ref,
                       y_ref, nst_ref, buf_sc, conv_sc, act_sc, *, bb, tl, bbk, rb, cbw):
    l = pl.program_id(1)
    m = bb * tl
    d = x_ref.shape[-1]
    c = wout_ref.shape[0]
    hist = CONV_WIDTH - 1
    head = 4 * SUBLANES
    off = head - hist

    @pl.when(l == 0)
    def _():
        buf_sc[:, off:head, :] = st_ref[...]

    x = x_ref[...].reshape(m, d)
    hn = _rms(x, g_ref[...]).astype(BF16)
    ag = _dot(hn, win_ref[...])
    u = ag[:, :c] * jax.nn.sigmoid(ag[:, c:])
    buf_sc[:, head:, :] = u.reshape(bb, tl, c)

    for b0 in range(0, bb, bbk):
        for r0 in range(0, tl, rb):
            lo = head + r0 - SUBLANES
            for c0 in range(0, c, cbw):
                cols = slice(c0, c0 + cbw)
                y = jnp.broadcast_to(bdw_ref[:, cols].reshape(1, 1, cbw), (bbk, rb, cbw))
                for s in range(SUBLANES):
                    z = None
                    for a in range(-(-CONV_WIDTH // SUBLANES)):
                        delay = SUBLANES * a + s
                        if delay >= CONV_WIDTH:
                            continue
                        w = jnp.concatenate([wdw_ref[hist - delay, :, cols]] * (rb // SUBLANES + 1), axis=0)[None]
                        term = w * buf_sc[b0:b0 + bbk, lo - SUBLANES * a:lo - SUBLANES * a + rb + SUBLANES, cols]
                        z = term if z is None else z + term
                    if s:
                        z = pltpu.roll(z, s, 1)
                    y = y + z[:, SUBLANES:, :]
                conv_sc[b0:b0 + bbk, r0:r0 + rb, cols] = y
            acc = conv_sc[b0:b0 + bbk, r0:r0 + rb, :]
            mu = jnp.mean(acc, axis=-1, keepdims=True)
            xc = acc - mu
            var = jnp.mean(xc * xc, axis=-1, keepdims=True)
            yn = xc * lax.rsqrt(var + EPS) * lng_ref[...].reshape(1, 1, c) + lnb_ref[...].reshape(1, 1, c)
            act_sc[b0:b0 + bbk, r0:r0 + rb, :] = (yn * jax.nn.sigmoid(yn)).astype(act_sc.dtype)
    y_ref[...] = (x + _dot(act_sc[...].reshape(m, c).astype(BF16), wout_ref[...])).reshape(bb, tl, d)

    new = buf_sc[:, tl + off:tl + head, :]

    @pl.when(l == pl.num_programs(1) - 1)
    def _():
        nst_ref[...] = new

    buf_sc[:, off:head, :] = new


def conv_mixer(x, g, state, win_bf, wdw, bdw, lng, lnb, wout_bf, *, bb, tl, bbk, rb):
    b, l, d = x.shape
    c = wout_bf.shape[0]
    hist = CONV_WIDTH - 1
    kern = functools.partial(_conv_mixer_kernel, bb=bb, tl=tl, bbk=bbk, rb=rb, cbw=256)
    return pl.pallas_call(
        kern,
        grid=(b // bb, l // tl),
        in_specs=[pl.BlockSpec((bb, tl, d), lambda i, j: (i, j, 0)),
                  _resident((1, d)),
                  pl.BlockSpec((bb, hist, c), lambda i, j: (i, 0, 0)),
                  _resident((d, 2 * c)),
                  _resident((CONV_WIDTH, SUBLANES, c)),
                  _resident((1, c)),
                  _resident((1, c)),
                  _resident((1, c)),
                  _resident((c, d))],
        out_specs=[pl.BlockSpec((bb, tl, d), lambda i, j: (i, j, 0)),
                   pl.BlockSpec((bb, hist, c), lambda i, j: (i, 0, 0))],
        out_shape=[jax.ShapeDtypeStruct((b, l, d), F32),
                   jax.ShapeDtypeStruct((b, hist, c), F32)],
        scratch_shapes=[pltpu.VMEM((bb, 4 * SUBLANES + tl, c), F32),
                        pltpu.VMEM((bb, tl, c), F32),
                        pltpu.VMEM((bb, tl, c), _mm_dtype(rb))],
        compiler_params=_params("parallel", "arbitrary"),
        name="conv_mixer",
    )(x, g.reshape(1, d), state, win_bf, jnp.broadcast_to(wdw[:, None, :], (CONV_WIDTH, SUBLANES, c)),
      bdw.reshape(1, c), lng.reshape(1, c), lnb.reshape(1, c), wout_bf)


def _gla_gate_kernel(g1_ref, w2_ref, b_ref, o_ref):
    z = _dot(g1_ref[...].astype(BF16), w2_ref[...]) + b_ref[...]
    o_ref[...] = _log_sigmoid(z) * (1.0 / GLA_GATE_NORM)


def gla_gate(h2, w2_pad_bf, b):
    m = h2.shape[0]
    dk = w2_pad_bf.shape[1]
    tm = min(m, 1024)
    g1_block = (2 * dk + 2 * GLA_HEADS * GLA_HEAD_V) // LANES
    return pl.pallas_call(
        _gla_gate_kernel,
        grid=(m // tm,),
        in_specs=[pl.BlockSpec((tm, LANES), lambda i: (i, g1_block)),
                  _resident((LANES, dk)),
                  _resident((1, dk))],
        out_specs=pl.BlockSpec((tm, dk), lambda i: (i, 0)),
        out_shape=jax.ShapeDtypeStruct((m, dk), F32),
        compiler_params=_params("parallel"),
        name="gla_gate",
    )(h2, w2_pad_bf, b.reshape(1, dk))


def _gla_rec_kernel(q_ref, k_ref, v_ref, g_ref, s0_ref, o_ref, s_ref, st_sc, *, tl, chunk):
    l = pl.program_id(1)
    mm = _mm_dtype(chunk)
    nh, hk, hv = GLA_HEADS, GLA_HEAD_K, GLA_HEAD_V

    @pl.when(l == 0)
    def _():
        for h in range(nh):
            st_sc[h] = s0_ref[0, h].T

    row = lax.broadcasted_iota(jnp.int32, (tl, tl), 0)
    col = lax.broadcasted_iota(jnp.int32, (tl, tl), 1)
    tril = ((row // chunk == col // chunk) & (col <= row)).astype(mm)
    gcum_all = _dot_exact_lhs01(tril, g_ref[0], mm)

    crow = lax.broadcasted_iota(jnp.int32, (chunk, chunk), 0)
    ccol = lax.broadcasted_iota(jnp.int32, (chunk, chunk), 1)
    causal = ccol <= crow

    for h in range(nh):
        kcols = slice(h * hk, (h + 1) * hk)
        vcols = slice(h * hv, (h + 1) * hv)
        q = q_ref[0, :, kcols] * (hk ** -0.5)
        k = k_ref[0, :, kcols]
        v = v_ref[0, :, vcols]
        gcum = gcum_all[:, kcols]
        st = st_sc[h]
        outs = []
        for c in range(tl // chunk):
            sl = slice(c * chunk, (c + 1) * chunk)
            gc = gcum[sl]
            gl = gc[chunk - 1:chunk]
            qe = (q[sl] * jnp.exp(gc)).astype(mm)
            ke = (k[sl] * jnp.exp(-gc)).astype(mm)
            kd = (k[sl] * jnp.exp(gl - gc)).astype(mm)
            vc = v[sl].astype(mm)
            scores = jnp.where(causal, _dot_nt(qe, ke), 0.0)
            outs.append(_dot_nt(qe, st.astype(mm)) + _dot(scores.astype(mm), vc))
            st = st * jnp.exp(gl) + _dot_tn(vc, kd)
        o_ref[0, :, vcols] = outs[0] if len(outs) == 1 else jnp.concatenate(outs, axis=0)
        st_sc[h] = st

    @pl.when(l == pl.num_programs(1) - 1)
    def _():
        for h in range(nh):
            s_ref[0, h] = st_sc[h].T


def gla_recurrence(h3, gk3, s0, *, tl, chunk):
    b, l, _ = h3.shape
    hk, hv, nh = GLA_HEAD_K, GLA_HEAD_V, GLA_HEADS
    dk, dv = nh * hk, nh * hv
    kern = functools.partial(_gla_rec_kernel, tl=tl, chunk=chunk)
    return pl.pallas_call(
        kern,
        grid=(b, l // tl),
        in_specs=[pl.BlockSpec((1, tl, dk), lambda i, j: (i, j, 0)),
                  pl.BlockSpec((1, tl, dk), lambda i, j: (i, j, 1)),
                  pl.BlockSpec((1, tl, dv), lambda i, j: (i, j, (2 * dk) // dv)),
                  pl.BlockSpec((1, tl, dk), lambda i, j: (i, j, 0)),
                  pl.BlockSpec((1, nh, hk, hv), lambda i, j: (i, 0, 0, 0))],
        out_specs=[pl.BlockSpec((1, tl, dv), lambda i, j: (i, j, 0)),
                   pl.BlockSpec((1, nh, hk, hv), lambda i, j: (i, 0, 0, 0))],
        out_shape=[jax.ShapeDtypeStruct((b, l, dv), F32),
                   jax.ShapeDtypeStruct((b, nh, hk, hv), F32)],
        scratch_shapes=[pltpu.VMEM((nh, hv, hk), F32)],
        compiler_params=_params("parallel", "arbitrary"),
        name="gla_recurrence",
    )(h3, h3, h3, gk3, s0)


def _gla_out_kernel(x_ref, o_ref, r_ref, gn_ref, w_ref, y_ref):
    o = o_ref[...]
    gn = gn_ref[...]
    parts = []
    for h in range(GLA_HEADS):
        parts.append(_rms(o[:, h * GLA_HEAD_V:(h + 1) * GLA_HEAD_V], gn))
    on = jnp.concatenate(parts, axis=-1)
    r = r_ref[...]
    gated = (on * (r * jax.nn.sigmoid(r))).astype(BF16)
    y_ref[...] = x_ref[...] + _dot(gated, w_ref[...])


def gla_out(x2, o2, h2, gn, wout_bf):
    m, d = x2.shape
    dv = o2.shape[1]
    tm = min(m, 512)
    r_block = (2 * GLA_HEADS * GLA_HEAD_K + dv) // dv
    return pl.pallas_call(
        _gla_out_kernel,
        grid=(m // tm,),
        in_specs=[pl.BlockSpec((tm, d), lambda i: (i, 0)),
                  pl.BlockSpec((tm, dv), lambda i: (i, 0)),
                  pl.BlockSpec((tm, dv), lambda i: (i, r_block)),
                  _resident((1, GLA_HEAD_V)),
                  _resident((dv, d))],
        out_specs=pl.BlockSpec((tm, d), lambda i: (i, 0)),
        out_shape=jax.ShapeDtypeStruct((m, d), F32),
        compiler_params=_params("parallel"),
        name="gla_out",
    )(x2, o2, h2, gn.reshape(1, GLA_HEAD_V), wout_bf)


def _head_norm(x, g2):
    lo = lax.broadcasted_iota(jnp.int32, (1, LANES), 1) < FOX_HEAD_DIM
    parts = []
    for j in range(x.shape[-1] // LANES):
        xb = x[:, j * LANES:(j + 1) * LANES]
        sq = xb * xb
        s_lo = jnp.sum(jnp.where(lo, sq, 0.0), axis=-1, keepdims=True)
        s_hi = jnp.sum(jnp.where(lo, 0.0, sq), axis=-1, keepdims=True)
        ms = jnp.where(lo, s_lo, s_hi) * (1.0 / FOX_HEAD_DIM)
        parts.append(xb * lax.rsqrt(ms + EPS) * g2)
    return jnp.concatenate(parts, axis=-1)


def _aug_select():
    sq = np.zeros((4 * LANES, FOX_HEADS * LANES), np.float32)
    sk = np.zeros((4 * LANES, FOX_HEADS * LANES), np.float32)
    for h in range(FOX_HEADS):
        a0 = h * LANES + (FOX_HEAD_DIM if h % 2 == 0 else 0)
        for i in range(3):
            sq[i * LANES + h, a0 + i] = 1.0
            sq[3 * LANES, a0 + 3 + i] = 1.0
            sk[3 * LANES, a0 + i] = 1.0
            sk[i * LANES + h, a0 + 3 + i] = -1.0
    return sq, sk


def _fox_prep_prompt_kernel(x_ref, g_ref, w_ref, qg_ref, kg_ref, bf_ref, sq_ref, sk_ref,
                            qa_ref, kn_ref, ka_ref, vo_ref, va_ref, lf_ref, og_ref, carry_sc, *, tl):
    l = pl.program_id(1)

    @pl.when(l == 0)
    def _():
        carry_sc[...] = jnp.zeros_like(carry_sc)

    d = FOX_HEADS * FOX_HEAD_DIM
    hp = _dot(_rms(x_ref[0], g_ref[...]).astype(BF16), w_ref[...])
    qn = _head_norm(hp[:, :d], qg_ref[...]) * (FOX_HEAD_DIM ** -0.5 * LOG2E)
    kn = _head_norm(hp[:, d:2 * d], kg_ref[...])
    v = hp[:, 2 * d:3 * d]
    kn_ref[0] = kn
    vo_ref[0] = v
    og_ref[0] = hp[:, 3 * d:4 * d]

    lf = _log_sigmoid(hp[:, 4 * d:] + bf_ref[...])
    lf_ref[0] = lf[:, :FOX_HEADS]
    row = lax.broadcasted_iota(jnp.int32, (tl, tl), 0)
    col = lax.broadcasted_iota(jnp.int32, (tl, tl), 1)
    c = _dot_exact_lhs01((col <= row).astype(BF16), lf, BF16) + carry_sc[...]
    carry_sc[...] = c[tl - 1:tl]
    pieces = jnp.concatenate(list(_split3(c * LOG2E)) + [jnp.ones((tl, LANES), F32)], axis=1).astype(BF16)
    qaug = _dot(pieces, sq_ref[...])
    kaug = _dot(pieces, sk_ref[...])

    lane = lax.broadcasted_iota(jnp.int32, (1, LANES), 1)
    lo = lane < FOX_HEAD_DIM
    for h in range(FOX_HEADS):
        src = slice((h // 2) * LANES, (h // 2 + 1) * LANES)
        dst = slice(h * LANES, (h + 1) * LANES)
        own = lo if h % 2 == 0 else jnp.logical_not(lo)
        vaug = jnp.where(lane == (FOX_HEAD_DIM if h % 2 == 0 else 0), 1.0, 0.0)
        qa_ref[0, :, dst] = jnp.where(own, qn[:, src], qaug[:, dst]).astype(BF16)
        ka_ref[0, :, dst] = jnp.where(own, kn[:, src], kaug[:, dst]).astype(BF16)
        va_ref[0, :, dst] = jnp.where(own, v[:, src], vaug).astype(BF16)


def fox_prep_prompt(x3, g, win_bf, qg, kg, bf, *, tl):
    b, l, dm = x3.shape
    d = FOX_HEADS * FOX_HEAD_DIM
    da = FOX_HEADS * LANES
    qg2 = jnp.tile(qg, 2).reshape(1, LANES)
    kg2 = jnp.tile(kg, 2).reshape(1, LANES)
    bf_pad = jnp.pad(bf, (0, LANES - FOX_HEADS)).reshape(1, LANES)
    sq, sk = _aug_select()
    row3 = lambda i, j: (i, j, 0)
    kern = functools.partial(_fox_prep_prompt_kernel, tl=tl)
    return pl.pallas_call(
        kern,
        grid=(b, l // tl),
        in_specs=[pl.BlockSpec((1, tl, dm), row3),
                  _resident((1, dm)),
                  _resident(win_bf.shape),
                  _resident((1, LANES)), _resident((1, LANES)), _resident((1, LANES)),
                  _resident(sq.shape), _resident(sk.shape)],
        out_specs=[pl.BlockSpec((1, tl, da), row3),
                   pl.BlockSpec((1, tl, d), row3),
                   pl.BlockSpec((1, tl, da), row3),
                   pl.BlockSpec((1, tl, d), row3),
                   pl.BlockSpec((1, tl, da), row3),
                   pl.BlockSpec((1, tl, FOX_HEADS), row3),
                   pl.BlockSpec((1, tl, d), row3)],
        out_shape=[jax.ShapeDtypeStruct((b, l, da), BF16),
                   jax.ShapeDtypeStruct((b, l, d), F32),
                   jax.ShapeDtypeStruct((b, l, da), BF16),
                   jax.ShapeDtypeStruct((b, l, d), F32),
                   jax.ShapeDtypeStruct((b, l, da), BF16),
                   jax.ShapeDtypeStruct((b, l, FOX_HEADS), F32),
                   jax.ShapeDtypeStruct((b, l, d), F32)],
        scratch_shapes=[pltpu.VMEM((1, LANES), F32)],
        compiler_params=_params("parallel", "arbitrary"),
        name="fox_prep_prompt",
    )(x3, g.reshape(1, dm), win_bf, qg2, kg2, bf_pad, jnp.asarray(sq, BF16), jnp.asarray(sk, BF16))


def _fox_prep_sample_kernel(q_ref, k_ref, v_ref, fl_ref, qg_ref, kg_ref, bf_ref, qo_ref, kn_ref, vo_ref, lf_ref):
    qo_ref[...] = _head_norm(q_ref[...], qg_ref[...]) * (FOX_HEAD_DIM ** -0.5)
    kn_ref[...] = _head_norm(k_ref[...], kg_ref[...])
    vo_ref[...] = v_ref[...]
    lf_ref[...] = _log_sigmoid(fl_ref[...] + bf_ref[...])[:, :FOX_HEADS]


def fox_prep_sample(h2, qg, kg, bf):
    m = h2.shape[0]
    d = FOX_HEADS * FOX_HEAD_DIM
    fl_block = (4 * d) // LANES
    qg2 = jnp.tile(qg, 2).reshape(1, LANES)
    kg2 = jnp.tile(kg, 2).reshape(1, LANES)
    bf_pad = jnp.pad(bf, (0, LANES - FOX_HEADS)).reshape(1, LANES)
    full = lambda i: (0, 0)
    return pl.pallas_call(
        _fox_prep_sample_kernel,
        grid=(1,),
        in_specs=[pl.BlockSpec((m, d), lambda i: (0, 0)),
                  pl.BlockSpec((m, d), lambda i: (0, 1)),
                  pl.BlockSpec((m, d), lambda i: (0, 2)),
                  pl.BlockSpec((m, LANES), lambda i: (0, fl_block)),
                  _resident((1, LANES)), _resident((1, LANES)), _resident((1, LANES))],
        out_specs=[pl.BlockSpec((m, d), full), pl.BlockSpec((m, d), full), pl.BlockSpec((m, d), full),
                   pl.BlockSpec((m, FOX_HEADS), full)],
        out_shape=[jax.ShapeDtypeStruct((m, d), F32), jax.ShapeDtypeStruct((m, d), F32),
                   jax.ShapeDtypeStruct((m, d), F32), jax.ShapeDtypeStruct((m, FOX_HEADS), F32)],
        compiler_params=_params("arbitrary"),
        name="fox_prep_sample",
    )(h2, h2, h2, h2, qg2, kg2, bf_pad)


def _flash_kernel(qt_ref, kt_ref, q_ref, k_ref, v_ref, o_ref, m_sc, acc_sc, *, t):
    step = pl.program_id(1)
    qi = qt_ref[step]
    ki = kt_ref[step]
    lo = lax.broadcasted_iota(jnp.int32, (1, LANES), 1) < FOX_HEAD_DIM

    @pl.when(ki == 0)
    def _():
        m_sc[...] = jnp.full_like(m_sc, -jnp.inf)
        acc_sc[...] = jnp.zeros_like(acc_sc)

    def head_update(h, diagonal):
        cols = pl.ds(pl.multiple_of(h * LANES, LANES), LANES)
        s = _dot_nt(q_ref[0, :, cols], k_ref[0, :, cols])
        if diagonal:
            row = lax.broadcasted_iota(jnp.int32, (t, t), 0)
            col = lax.broadcasted_iota(jnp.int32, (t, t), 1)
            s = jnp.where(col <= row, s, -jnp.inf)
        m_prev = m_sc[h]
        m_new = jnp.maximum(m_prev, jnp.max(s, axis=-1, keepdims=True))
        alpha = jnp.exp2(m_prev - m_new)
        p = jnp.exp2(s - jnp.concatenate([m_new] * (t // LANES), axis=1))
        acc_sc[h] = acc_sc[h] * alpha + _dot(p.astype(BF16), v_ref[0, :, cols])
        m_sc[h] = m_new

    @pl.when(ki < qi)
    def _():
        def body(j, carry):
            head_update(2 * j, False)
            head_update(2 * j + 1, False)
            return carry

        lax.fori_loop(0, FOX_HEADS // 2, body, 0, unroll=8)

    @pl.when(ki == qi)
    def _():
        def body(j, carry):
            head_update(2 * j, True)
            head_update(2 * j + 1, True)
            a0 = acc_sc[2 * j]
            a1 = acc_sc[2 * j + 1]
            out = jnp.where(lo, a0 / a0[:, FOX_HEAD_DIM:FOX_HEAD_DIM + 1], a1 / a1[:, 0:1])
            o_ref[0, :, pl.ds(pl.multiple_of(j * LANES, LANES), LANES)] = out
            return carry

        lax.fori_loop(0, FOX_HEADS // 2, body, 0, unroll=2)


def fox_flash(qa, ka, va, *, t):
    b, l, da = qa.shape
    d = FOX_HEADS * FOX_HEAD_DIM
    n = l // t
    qi = np.concatenate([np.full(i + 1, i, np.int32) for i in range(n)])
    ki = np.concatenate([np.arange(i + 1, dtype=np.int32) for i in range(n)])
    kern = functools.partial(_flash_kernel, t=t)
    grid_spec = pltpu.PrefetchScalarGridSpec(
        num_scalar_prefetch=2,
        grid=(b, len(qi)),
        in_specs=[pl.BlockSpec((1, t, da), lambda i, s, qt, kt: (i, qt[s], 0)),
                  pl.BlockSpec((1, t, da), lambda i, s, qt, kt: (i, kt[s], 0)),
                  pl.BlockSpec((1, t, da), lambda i, s, qt, kt: (i, kt[s], 0))],
        out_specs=pl.BlockSpec((1, t, d), lambda i, s, qt, kt: (i, qt[s], 0)),
        scratch_shapes=[pltpu.VMEM((FOX_HEADS, t, LANES), F32),
                        pltpu.VMEM((FOX_HEADS, t, LANES), F32)])
    return pl.pallas_call(
        kern,
        grid_spec=grid_spec,
        out_shape=jax.ShapeDtypeStruct((b, l, d), F32),
        compiler_params=_params("parallel", "arbitrary"),
        name="fox_flash",
    )(jnp.asarray(qi), jnp.asarray(ki), qa, ka, va)


def _decode_kernel(pt_ref, q_ref, kn_ref, vn_ref, lfn_ref, *refs, lq, page, pg):
    del pt_ref
    kt_refs, vt_refs, lft_refs = refs[:pg], refs[pg:2 * pg], refs[2 * pg:3 * pg]
    o_ref = refs[3 * pg]
    qbd_sc, m_sc, l_sc, acc_sc, cq_sc, carry_sc = refs[3 * pg + 1:]
    p = pl.program_id(1)
    nh, hd = FOX_HEADS, FOX_HEAD_DIM
    rows = nh * lq
    d = nh * hd

    uj = lax.broadcasted_iota(jnp.int32, (page, 2 * page), 0)
    us = lax.broadcasted_iota(jnp.int32, (page, 2 * page), 1)
    u2 = ((uj > us) | (us >= page)).astype(BF16)

    def later_sums(x, keep):
        out = jnp.zeros((x.shape[0], 2 * page), F32)
        for piece in _split3(x):
            if keep is not None:
                piece = jnp.where(keep, piece, 0.0)
            out = out + _dot(piece.astype(BF16), u2)
        return out

    def head_rows(x):
        return jnp.concatenate([jnp.broadcast_to(x[h:h + 1], (lq, x.shape[1])) for h in range(nh)], axis=0)

    @pl.when(p == 0)
    def _():
        qt = jnp.concatenate([q_ref[0]] * nh, axis=0)
        rr = lax.broadcasted_iota(jnp.int32, (rows, d), 0)
        cc = lax.broadcasted_iota(jnp.int32, (rows, d), 1)
        qbd_sc[...] = jnp.where(rr // lq == cc // hd, qt, 0.0)

        pad = page - lq
        kn = jnp.concatenate([kn_ref[0], jnp.zeros((pad, d), F32)], axis=0)
        vn = jnp.concatenate([vn_ref[0], jnp.zeros((pad, d), F32)], axis=0)
        lfn = jnp.concatenate([lfn_ref[0], jnp.zeros((pad, nh), F32)], axis=0)
        er = lax.broadcasted_iota(jnp.int32, (rows, nh), 0)
        ec = lax.broadcasted_iota(jnp.int32, (rows, nh), 1)
        expand = (er // lq == ec).astype(BF16)
        t = lax.broadcasted_iota(jnp.int32, (rows, page), 0) % lq
        j = lax.broadcasted_iota(jnp.int32, (rows, page), 1)
        sums = jnp.zeros((rows, 2 * page), F32)
        for piece in _split3(lfn):
            e = jnp.where(j <= t, _dot_nt(expand, piece.astype(BF16)), 0.0)
            sums = sums + _dot(e.astype(BF16), u2)
        cq_sc[...] = sums[:, page:]
        s = jnp.where(j <= t, _dot_nt(qbd_sc[...], kn) + sums[:, :page], -jnp.inf)
        m_cur = jnp.max(s, axis=-1, keepdims=True)
        pexp = jnp.exp(s - m_cur)
        m_sc[...] = jnp.broadcast_to(m_cur, (rows, LANES))
        l_sc[...] = jnp.broadcast_to(jnp.sum(pexp, axis=-1, keepdims=True), (rows, LANES))
        acc_sc[...] = _dot(pexp, vn)
        carry_sc[...] = jnp.zeros_like(carry_sc)

    lft = jnp.concatenate([r[0] for r in lft_refs], axis=0)
    sums = later_sums(lft, None)
    carry = carry_sc[...]
    tiles = []
    for i in range(pg):
        blk = slice(i * nh, (i + 1) * nh)
        bias = head_rows(sums[blk, :page] + carry) + cq_sc[...]
        carry = carry + sums[blk, page:]
        tiles.append(_dot(qbd_sc[...], kt_refs[i][0]) + bias)
    carry_sc[...] = carry
    s = jnp.concatenate(tiles, axis=1)
    m_prev = m_sc[...]
    m_new = jnp.maximum(m_prev, jnp.max(s, axis=-1, keepdims=True))
    alpha = jnp.exp(m_prev - m_new)
    pexp = jnp.exp(s - jnp.concatenate([m_new] * pg, axis=1))
    l_sc[...] = alpha * l_sc[...] + jnp.sum(pexp, axis=-1, keepdims=True)
    m_sc[...] = m_new
    pv = _dot_nt(pexp[:, :page], vt_refs[0][0])
    for i in range(1, pg):
        pv = pv + _dot_nt(pexp[:, i * page:(i + 1) * page], vt_refs[i][0])
    for c in range(d // LANES):
        cols = slice(c * LANES, (c + 1) * LANES)
        acc_sc[:, cols] = acc_sc[:, cols] * alpha + pv[:, cols]

    @pl.when(p == pl.num_programs(1) - 1)
    def _():
        rr = lax.broadcasted_iota(jnp.int32, (rows, d), 0)
        cc = lax.broadcasted_iota(jnp.int32, (rows, d), 1)
        inv = 1.0 / l_sc[...]
        parts = [acc_sc[:, c * LANES:(c + 1) * LANES] * inv for c in range(d // LANES)]
        on = jnp.where(rr // lq == cc // hd, jnp.concatenate(parts, axis=-1), 0.0)
        o_ref[0] = jnp.sum(on.reshape(nh, lq, d), axis=0)


def fox_decode(q3, kn3, vn3, lfn3, kpool_t, vpool_t, lfpool_t, page_table, *, pg):
    b, lq, d = q3.shape
    nh = FOX_HEADS
    n_pages = page_table.shape[1]
    page = lfpool_t.shape[2]
    rows = nh * lq
    assert page == LANES and rows == LANES and n_pages % pg == 0

    def new_map(i, p, pt):
        return (i, 0, 0)

    def page_map(k):
        return lambda i, p, pt: (pt[i, n_pages - 1 - (p * pg + k)], 0, 0)

    kern = functools.partial(_decode_kernel, lq=lq, page=page, pg=pg)
    grid_spec = pltpu.PrefetchScalarGridSpec(
        num_scalar_prefetch=1,
        grid=(b, n_pages // pg),
        in_specs=[pl.BlockSpec((1, lq, d), new_map),
                  pl.BlockSpec((1, lq, d), new_map),
                  pl.BlockSpec((1, lq, d), new_map),
                  pl.BlockSpec((1, lq, nh), new_map)]
                 + [pl.BlockSpec((1, d, page), page_map(k)) for k in range(pg)]
                 + [pl.BlockSpec((1, d, page), page_map(k)) for k in range(pg)]
                 + [pl.BlockSpec((1, nh, page), page_map(k)) for k in range(pg)],
        out_specs=pl.BlockSpec((1, lq, d), new_map),
        scratch_shapes=[pltpu.VMEM((rows, d), F32),
                        pltpu.VMEM((rows, LANES), F32),
                        pltpu.VMEM((rows, LANES), F32),
                        pltpu.VMEM((rows, d), F32),
                        pltpu.VMEM((rows, LANES), F32),
                        pltpu.VMEM((nh, LANES), F32)])
    return pl.pallas_call(
        kern,
        grid_spec=grid_spec,
        out_shape=jax.ShapeDtypeStruct((b, lq, d), F32),
        compiler_params=_params("parallel", "arbitrary"),
        name="fox_decode",
    )(page_table, q3, kn3, vn3, lfn3, *([kpool_t] * pg), *([vpool_t] * pg), *([lfpool_t] * pg))


def _fox_out_kernel(x_ref, o_ref, og_ref, w_ref, y_ref):
    gated = (o_ref[...] * jax.nn.sigmoid(og_ref[...])).astype(BF16)
    y_ref[...] = x_ref[...] + _dot(gated, w_ref[...])


def fox_out(x2, o2, gate2, gate_block, wout_bf):
    m, d = x2.shape
    tm = min(m, 512)
    return pl.pallas_call(
        _fox_out_kernel,
        grid=(m // tm,),
        in_specs=[pl.BlockSpec((tm, d), lambda i: (i, 0)),
                  pl.BlockSpec((tm, d), lambda i: (i, 0)),
                  pl.BlockSpec((tm, d), lambda i: (i, gate_block)),
                  _resident((d, d))],
        out_specs=pl.BlockSpec((tm, d), lambda i: (i, 0)),
        out_shape=jax.ShapeDtypeStruct((m, d), F32),
        compiler_params=_params("parallel"),
        name="fox_out",
    )(x2, o2, gate2, wout_bf)


def _tiles(x):
    b, l, _ = x.shape
    if l >= 512:
        return 1, 512
    return b, l


def _conv_layer(x, g, state, prm):
    bb, tl = _tiles(x)
    tl = min(tl, 256)
    bbk, rb = (1, 64) if tl >= 64 else (64 // tl, tl)
    return conv_mixer(x, g, state, *prm, bb=bb, tl=tl, bbk=bbk, rb=rb)


def _ffn_layer(x, g, state, prm):
    bb, tl = _tiles(x)
    return conv_ffn(x, g, state, *prm, bb=bb, tl=tl, cw=FFN_CHUNK)


def _gla_layer(x, g, s0, prm):
    win_bf, w2_bf, b_gk, gn, wout_bf = prm
    b, l, d = x.shape
    x2 = x.reshape(b * l, d)
    h2 = norm_matmul(x2, g, win_bf)
    gk2 = gla_gate(h2, w2_bf, b_gk)
    chunk = GLA_CHUNK if l % GLA_CHUNK == 0 else l
    tl = 4 * chunk if l % (4 * chunk) == 0 else chunk
    o3, s = gla_recurrence(h2.reshape(b, l, -1), gk2.reshape(b, l, -1), s0, tl=tl, chunk=chunk)
    y2 = gla_out(x2, o3.reshape(b * l, -1), h2, gn, wout_bf)
    return y2.reshape(b, l, d), s


def _fox_layer(x, g, cache, prm):
    win_bf, b_f, qg, kg, wout_bf = prm
    b, l, d = x.shape
    x2 = x.reshape(b * l, d)
    if cache is None:
        qa, kn, ka, vo, va, lf, og = fox_prep_prompt(x, g, win_bf, qg, kg, b_f, tl=512)
        o3 = fox_flash(qa, ka, va, t=512)
        y2 = fox_out(x2, o3.reshape(b * l, d), og.reshape(b * l, d), 0, wout_bf)
    else:
        kpool_t, vpool_t, lfpool_t, page_table = cache
        h2 = norm_matmul(x2, g, win_bf)
        q2, kn, vo, lf = fox_prep_sample(h2, qg, kg, b_f)
        o3 = fox_decode(q2.reshape(b, l, d), kn.reshape(b, l, d), vo.reshape(b, l, d), lf.reshape(b, l, FOX_HEADS),
                        kpool_t, vpool_t, lfpool_t, page_table, pg=8)
        y2 = fox_out(x2, o3.reshape(b * l, d), h2, 3, wout_bf)
    shp = (b, l, FOX_HEADS, FOX_HEAD_DIM)
    return y2.reshape(b, l, d), kn.reshape(shp), vo.reshape(shp), lf.reshape(b, l, FOX_HEADS)


def _pad_cols(w, n):
    return jnp.pad(w, ((0, 0), (0, n - w.shape[1])))


def kernel(x_prompt, x_sample, state_conv, state_gla, cache_fox_k, cache_fox_v, cache_fox_logf, state_ffn_conv, page_table,
           norm_mix_g, norm_ffn_g, w_conv_in, w_conv_dw, b_conv_dw, conv_ln_g, conv_ln_b, w_conv_out,
           w_gla_in, w_gla_gk2, b_gla_gk, gla_norm_g, w_gla_out,
           w_fox_in, b_fox_f, fox_qn_g, fox_kn_g, w_fox_out,
           w_ffn_up, w_ffn_dw, b_ffn_dw, w_ffn_down):
    depth = norm_mix_g.shape[0]
    bp = x_prompt.shape[0]
    d = x_prompt.shape[-1]
    xp, xs = x_prompt, x_sample
    conv_p, conv_s, gla_p, gla_s = [], [], [], []
    fk_p, fk_s, fv_p, fv_s, fl_p, fl_s = [], [], [], [], [], []
    ffn_p, ffn_s = [], []
    for i in range(depth):
        m, j = i % 3, i // 3
        if m == 0:
            prm = (w_conv_in[j].astype(BF16), w_conv_dw[j], b_conv_dw[j], conv_ln_g[j], conv_ln_b[j],
                   w_conv_out[j].astype(BF16))
            zero = jnp.zeros((bp,) + state_conv.shape[2:], F32)
            xp, st_p = _conv_layer(xp, norm_mix_g[i], zero, prm)
            xs, st_s = _conv_layer(xs, norm_mix_g[i], state_conv[j], prm)
            conv_p.append(st_p)
            conv_s.append(st_s)
        elif m == 1:
            n_in = w_gla_in.shape[-1]
            n_pad = -(-n_in // (5 * LANES)) * (5 * LANES)
            w2 = jnp.pad(w_gla_gk2[j], ((0, LANES - GLA_GATE_RANK), (0, 0))).astype(BF16)
            prm = (_pad_cols(w_gla_in[j], n_pad).astype(BF16), w2, b_gla_gk[j], gla_norm_g[j],
                   w_gla_out[j].astype(BF16))
            zero = jnp.zeros((bp,) + state_gla.shape[2:], F32)
            xp, st_p = _gla_layer(xp, norm_mix_g[i], zero, prm)
            xs, st_s = _gla_layer(xs, norm_mix_g[i], state_gla[j], prm)
            gla_p.append(st_p)
            gla_s.append(st_s)
        else:
            n_pad = 4 * d + LANES
            prm = (_pad_cols(w_fox_in[j], n_pad).astype(BF16), b_fox_f[j], fox_qn_g[j], fox_kn_g[j],
                   w_fox_out[j].astype(BF16))
            n_pool, page = cache_fox_k.shape[1], cache_fox_k.shape[2]
            cache = (jnp.transpose(cache_fox_k[j], (0, 2, 3, 1)).reshape(n_pool, d, page),
                     jnp.transpose(cache_fox_v[j], (0, 2, 3, 1)).reshape(n_pool, d, page),
                     jnp.transpose(cache_fox_logf[j], (0, 2, 1)), page_table)
            xp, kp, vp, lp = _fox_layer(xp, norm_mix_g[i], None, prm)
            xs, kn, vn, ln = _fox_layer(xs, norm_mix_g[i], cache, prm)
            fk_p.append(kp)
            fk_s.append(kn)
            fv_p.append(vp)
            fv_s.append(vn)
            fl_p.append(lp)
            fl_s.append(ln)
        fprm = (w_ffn_up[i].astype(BF16), w_ffn_dw[i], b_ffn_dw[i], w_ffn_down[i].astype(BF16))
        zero = jnp.zeros((bp,) + state_ffn_conv.shape[2:], F32)
        xp, st_p = _ffn_layer(xp, norm_ffn_g[i], zero, fprm)
        xs, st_s = _ffn_layer(xs, norm_ffn_g[i], state_ffn_conv[i], fprm)
        ffn_p.append(st_p)
        ffn_s.append(st_s)
    return (xp, xs, jnp.stack(conv_p), jnp.stack(conv_s), jnp.stack(gla_p), jnp.stack(gla_s),
            jnp.stack(fk_p), jnp.stack(fk_s), jnp.stack(fv_p), jnp.stack(fv_s), jnp.stack(fl_p), jnp.stack(fl_s),
            jnp.stack(ffn_p), jnp.stack(ffn_s))
```
